```python
import math
import jax
import jax.numpy as jnp
from jax import lax
import numpy as np

D_MODEL = 1024
BATCH = 4
SEQ = 8192
DEPTH = 2

GDN_HEADS = 4
GDN_DK = 128
GDN_DV = 128
GDN_CONV = 5
GDN_CHUNK = 64
MLA_HEADS = 8
MLA_NOPE = 64
MLA_ROPE = 32
MLA_V = 64
MLA_Q_LORA = 384
MLA_KV_LORA = 256
MLA_QBLOCK = 128
ROPE_THETA = 10000.0
D_FF = 2816
RES_HALF = 0.5
N_BRANCH = 2
EPS = 1e-6

GDN_QK = GDN_HEADS * GDN_DK
GDN_VW = GDN_HEADS * GDN_DV
MLA_QK = MLA_NOPE + MLA_ROPE
MLA_OUT = MLA_HEADS * MLA_V
IN_SPLITS = (GDN_QK, GDN_QK, GDN_VW, GDN_VW, 2 * GDN_HEADS, 2 * GDN_HEADS,
             MLA_Q_LORA, MLA_KV_LORA, MLA_ROPE, N_BRANCH * D_MODEL)
D_IN = sum(IN_SPLITS)

kernel_name = "hybrid_gdn_mla_macaron_encoder"


def rmsnorm(x, w):
    xf = x.astype(jnp.float32)
    y = xf * lax.rsqrt(jnp.mean(xf * xf, axis=-1, keepdims=True) + EPS)
    return (y * w.astype(jnp.float32)).astype(x.dtype)


def l2norm(x):
    xf = x.astype(jnp.float32)
    return (xf * lax.rsqrt(jnp.sum(xf * xf, axis=-1, keepdims=True) + EPS)).astype(x.dtype)


def swiglu(x, w_gate, w_up, w_down):
    return (jax.nn.silu(x @ w_gate) * (x @ w_up)) @ w_down


def split_cols(t):
    parts, start = [], 0
    for width in IN_SPLITS:
        parts.append(t[..., start:start + width])
        start += width
    return parts


def centred_short_conv(x, w):
    pad = GDN_CONV // 2
    y = lax.conv_general_dilated(x, w[:, None, :].astype(x.dtype), (1,), [(pad, pad)],
                                 dimension_numbers=("NWC", "WIO", "NWC"),
                                 feature_group_count=x.shape[-1])
    return jax.nn.silu(y)


def rope(x, cos, sin):
    half = x.shape[-1] // 2
    x1, x2 = x[..., :half], x[..., half:]
    return jnp.concatenate([x1 * cos - x2 * sin, x2 * cos + x1 * sin], axis=-1).astype(x.dtype)


def gated_delta_chunked(q, k, v, g, beta):
    out_dtype = v.dtype
    f32 = jnp.float32
    q, k, v, g, beta = (t.astype(f32) for t in (q, k, v, g, beta))
    B, S, H, Dk = q.shape
    Dv = v.shape[-1]
    C = GDN_CHUNK
    N = S // C

    def to_chunks(t):
        return jnp.moveaxis(t.reshape((B, N, C, H) + t.shape[3:]), (1, 3), (0, 2))

    qc, kc, vc, bc = to_chunks(q), to_chunks(k), to_chunks(v), to_chunks(beta)
    gc = jnp.cumsum(to_chunks(g), axis=-1)
    incl = jnp.tril(jnp.ones((C, C), bool))
    strict = jnp.tril(jnp.ones((C, C), bool), -1)
    decay = jnp.exp(jnp.where(incl, gc[..., :, None] - gc[..., None, :], -jnp.inf))
    kb = kc * bc[..., None]
    lower = jnp.where(strict, jnp.einsum("nbhik,nbhjk->nbhij", kb, kc) * decay, 0.0)
    unit = jnp.eye(C, dtype=f32) + lower
    rhs = jnp.concatenate([vc * bc[..., None], kb * jnp.exp(gc)[..., None]], axis=-1)
    uw = lax.linalg.triangular_solve(unit, rhs, left_side=True, lower=True, unit_diagonal=True)
    u, w = uw[..., :Dv], uw[..., Dv:]
    intra = jnp.where(incl, jnp.einsum("nbhik,nbhjk->nbhij", qc, kc) * decay, 0.0)

    def step(state, xs):
        q_i, k_i, u_i, w_i, g_i, a_i = xs
        v_new = u_i - jnp.einsum("bhck,bhkv->bhcv", w_i, state)
        o_i = (jnp.einsum("bhck,bhkv->bhcv", q_i * jnp.exp(g_i)[..., None], state)
               + jnp.einsum("bhij,bhjv->bhiv", a_i, v_new))
        g_last = g_i[..., -1:]
        state = (state * jnp.exp(g_last)[..., None]
                 + jnp.einsum("bhck,bhcv->bhkv", k_i * jnp.exp(g_last - g_i)[..., None], v_new))
        return state, o_i

    state0 = jnp.zeros((B, H, Dk, Dv), f32)
    _, o = lax.scan(step, state0, (qc, kc, u, w, gc, intra))
    return jnp.moveaxis(o, (0, 2), (1, 3)).reshape(B, S, H, Dv).astype(out_dtype)


def gdn_branch(q, k, v, z, b, a, conv_w, A_log, dt_bias, norm_w, w_proj):
    B, S, _ = q.shape
    f32 = jnp.float32
    qkv = centred_short_conv(jnp.concatenate([q, k, v], axis=-1), conv_w)
    qh = l2norm(qkv[..., :GDN_QK].reshape(B, S, GDN_HEADS, GDN_DK)) * GDN_DK ** -0.5
    kh = l2norm(qkv[..., GDN_QK:2 * GDN_QK].reshape(B, S, GDN_HEADS, GDN_DK))
    vh = qkv[..., 2 * GDN_QK:].reshape(B, S, GDN_HEADS, GDN_DV)
    beta = jax.nn.sigmoid(b.astype(f32)).reshape(B, S, 2, GDN_HEADS)
    g = -jnp.exp(A_log.astype(f32)) * jax.nn.softplus(
        a.astype(f32).reshape(B, S, 2, GDN_HEADS) + dt_bias.astype(f32))
    o_fwd = gated_delta_chunked(qh, kh, vh, g[:, :, 0], beta[:, :, 0])
    flip = lambda t: jnp.flip(t, axis=1)
    o_bwd = flip(gated_delta_chunked(flip(qh), flip(kh), flip(vh), flip(g[:, :, 1]), flip(beta[:, :, 1])))
    o = rmsnorm(o_fwd + o_bwd, norm_w) * jax.nn.silu(z.reshape(B, S, GDN_HEADS, GDN_DV))
    return o.reshape(B, S, GDN_VW) @ w_proj


def mla_branch(c_q, c_kv, k_rope, cos, sin, q_norm, w_uq, kv_norm, w_ukv, w_proj):
    B, S, _ = c_q.shape
    scale = MLA_QK ** -0.5
    q = (rmsnorm(c_q, q_norm) @ w_uq).reshape(B, S, MLA_HEADS, MLA_QK)
    q_nope = q[..., :MLA_NOPE] * scale
    q_rope = rope(q[..., MLA_NOPE:], cos[:, :, None, :], sin[:, :, None, :]) * scale
    kv = (rmsnorm(c_kv, kv_norm) @ w_ukv).reshape(B, S, MLA_HEADS, MLA_NOPE + MLA_V)
    k_nope, v = kv[..., :MLA_NOPE], kv[..., MLA_NOPE:]
    k_r = rope(k_rope, cos, sin)
    nb = S // MLA_QBLOCK

    def blocks(t):
        return jnp.moveaxis(t.reshape((B, nb, MLA_QBLOCK) + t.shape[2:]), 1, 0)

    def attend(qb):
        qn, qr = qb
        s = (jnp.einsum("bqhd,bkhd->bhqk", qn, k_nope)
             + jnp.einsum("bqhr,bkr->bhqk", qr, k_r))
        p = jax.nn.softmax(s.astype(jnp.float32), axis=-1).astype(v.dtype)
        return jnp.einsum("bhqk,bkhd->bqhd", p, v)

    o = lax.map(attend, (blocks(q_nope), blocks(q_rope)))
    o = jnp.moveaxis(o, 0, 1).reshape(B, S, MLA_OUT)
    return o @ w_proj


def setup_inputs(seed: int = 0) -> dict:
    key = jax.random.key(seed)
    ks = jax.random.split(key, 32)
    f32 = jnp.float32

    def dense(k, fan_in, *shape):
        return jax.random.normal(k, (DEPTH,) + shape, f32) * fan_in ** -0.5

    def gain(k, *shape):
        return 1.0 + 0.02 * jax.random.normal(k, shape, f32)

    x = jax.random.normal(ks[0], (BATCH, SEQ, D_MODEL), f32)
    positions = (jax.random.randint(ks[1], (BATCH, 1), 0, 1024, jnp.int32)
                 + jnp.arange(SEQ, dtype=jnp.int32)[None, :])
    gdn_A_log = jnp.log(jax.random.uniform(ks[2], (DEPTH, 2, GDN_HEADS), f32, 1.0, 16.0))
    dt = jnp.exp(jax.random.uniform(ks[3], (DEPTH, 2, GDN_HEADS), f32, math.log(1e-3), math.log(1e-1)))
    gdn_dt_bias = dt + jnp.log(-jnp.expm1(-dt))
    return {
        "x": x,
        "positions": positions,
        "norm_ffn1": gain(ks[4], DEPTH, D_MODEL),
        "ffn1_w_gate": dense(ks[5], D_MODEL, D_MODEL, D_FF),
        "ffn1_w_up": dense(ks[6], D_MODEL, D_MODEL, D_FF),
        "ffn1_w_down": dense(ks[7], D_FF, D_FF, D_MODEL),
        "norm_mix": gain(ks[8], DEPTH, D_MODEL),
        "w_in": dense(ks[9], D_MODEL, D_MODEL, D_IN),
        "gdn_conv": dense(ks[10], GDN_CONV, GDN_CONV, 2 * GDN_QK + GDN_VW),
        "gdn_A_log": gdn_A_log,
        "gdn_dt_bias": gdn_dt_bias,
        "gdn_norm": gain(ks[11], DEPTH, GDN_DV),
        "gdn_proj": dense(ks[12], GDN_VW, GDN_VW, D_MODEL),
        "mla_q_norm": gain(ks[13], DEPTH, MLA_Q_LORA),
        "mla_w_uq": dense(ks[14], MLA_Q_LORA, MLA_Q_LORA, MLA_HEADS * MLA_QK),
        "mla_kv_norm": gain(ks[15], DEPTH, MLA_KV_LORA),
        "mla_w_ukv": dense(ks[16], MLA_KV_LORA, MLA_KV_LORA, MLA_HEADS * (MLA_NOPE + MLA_V)),
        "mla_proj": dense(ks[17], MLA_OUT, MLA_OUT, D_MODEL),
        "w_out": dense(ks[18], D_MODEL, D_MODEL, D_MODEL),
        "norm_ffn2": gain(ks[19], DEPTH, D_MODEL),
        "ffn2_w_gate": dense(ks[20], D_MODEL, D_MODEL, D_FF),
        "ffn2_w_up": dense(ks[21], D_MODEL, D_MODEL, D_FF),
        "ffn2_w_down": dense(ks[22], D_FF, D_FF, D_MODEL),
        "final_norm": gain(ks[23], D_MODEL),
    }


def reference(x, positions, norm_ffn1, ffn1_w_gate, ffn1_w_up, ffn1_w_down, norm_mix, w_in,
              gdn_conv, gdn_A_log, gdn_dt_bias, gdn_norm, gdn_proj, mla_q_norm, mla_w_uq,
              mla_kv_norm, mla_w_ukv, mla_proj, w_out, norm_ffn2, ffn2_w_gate, ffn2_w_up,
              ffn2_w_down, final_norm):
    B, S, D = x.shape
    inv_freq = jnp.power(ROPE_THETA, -jnp.arange(0, MLA_ROPE, 2, dtype=jnp.float32) / MLA_ROPE)
    ang = positions.astype(jnp.float32)[..., None] * inv_freq
    cos, sin = jnp.cos(ang), jnp.sin(ang)
    for l in range(DEPTH):
        h = rmsnorm(x, norm_ffn1[l])
        x = x + RES_HALF * swiglu(h, ffn1_w_gate[l], ffn1_w_up[l], ffn1_w_down[l])
        h = rmsnorm(x, norm_mix[l])
        gq, gk, gv, gz, gb, ga, c_q, c_kv, k_rope, gate_logits = split_cols(h @ w_in[l])
        y_a = gdn_branch(gq, gk, gv, gz, gb, ga, gdn_conv[l], gdn_A_log[l], gdn_dt_bias[l],
                         gdn_norm[l], gdn_proj[l])
        y_b = mla_branch(c_q, c_kv, k_rope, cos, sin, mla_q_norm[l], mla_w_uq[l],
                         mla_kv_norm[l], mla_w_ukv[l], mla_proj[l])
        gates = jax.nn.sigmoid(gate_logits.astype(jnp.float32)).astype(x.dtype).reshape(B, S, N_BRANCH, D)
        x = x + (gates[:, :, 0] * y_a + gates[:, :, 1] * y_b) @ w_out[l]
        h = rmsnorm(x, norm_ffn2[l])
        x = x + RES_HALF * swiglu(h, ffn2_w_gate[l], ffn2_w_up[l], ffn2_w_down[l])
    return rmsnorm(x, final_norm)
```

```python
import functools

import jax
import jax.numpy as jnp
from jax import lax
from jax.experimental import pallas as pl
from jax.experimental.pallas import tpu as pltpu

F32 = jnp.float32
BF16 = jnp.bfloat16

EPS = 1e-6
RES_HALF = 0.5
GDN_HEADS = 4
GDN_D = 128
GDN_CONV = 5
GDN_CHUNK = 64
MLA_HEADS = 8
MLA_NOPE = 64
MLA_ROPE = 32
MLA_V = 64
MLA_Q_LORA = 384
MLA_KV_LORA = 256
ROPE_THETA = 10000.0
HEAD_LANES = 128
V_ROWS = 80
LOG2E = 1.4426950408889634
NEG_BIG = -1e30

VMEM_LIMIT_BYTES = 56 * 1024 * 1024

NT_DIMS = (((1,), (1,)), ((), ()))


def _params():
    return pltpu.CompilerParams(dimension_semantics=None, vmem_limit_bytes=VMEM_LIMIT_BYTES)


def _grid_params(n):
    return pltpu.CompilerParams(dimension_semantics=("arbitrary",) * n, vmem_limit_bytes=VMEM_LIMIT_BYTES)


def _const_spec(shape):
    nd = len(shape)
    return pl.BlockSpec(shape, lambda *_: (0,) * nd, pipeline_mode=pl.Buffered(1))


def _rms(x, w):
    return x * lax.rsqrt(jnp.mean(x * x, axis=-1, keepdims=True) + EPS) * w


def _silu(x):
    return x * jax.nn.sigmoid(x)


def _softplus(x):
    return jnp.maximum(x, 0.0) + jnp.log1p(jnp.exp(-jnp.abs(x)))


def _dot(a, b):
    return jnp.dot(a, b, preferred_element_type=F32)


def _ffn_body(x_ref, nw_ref, wg_ref, wu_ref, wd_ref, fw_ref, o_ref, *, final_norm):
    x = x_ref[...]
    hb = _rms(x, nw_ref[...]).astype(BF16)
    g = _dot(hb, wg_ref[...])
    u = _dot(hb, wu_ref[...])
    a = (_silu(g) * u).astype(BF16)
    y = x + RES_HALF * _dot(a, wd_ref[...])
    if final_norm:
        y = _rms(y, fw_ref[...])
    o_ref[...] = y


def _ffn(x, nw, wg, wu, wd, fw, *, final_norm, tm):
    t, d = x.shape
    ff = wg.shape[1]
    row = pl.BlockSpec((tm, d), lambda i: (i, 0))
    return pl.pallas_call(
        functools.partial(_ffn_body, final_norm=final_norm),
        out_shape=jax.ShapeDtypeStruct((t, d), F32),
        grid=(t // tm,),
        in_specs=[row, _const_spec((1, d)), _const_spec((d, ff)), _const_spec((d, ff)),
                  _const_spec((ff, d)), _const_spec((1, d))],
        out_specs=row,
        compiler_params=_grid_params(1),
        name="ffn",
    )(x, nw, wg, wu, wd, fw)


def _rope_body(pos_ref, post_ref, frow_ref, srow_ref, fcol_ref, scol_ref, cos_ref, sin_ref, cost_ref, sint_ref):
    ang = pos_ref[...].astype(F32) * frow_ref[...]
    cos_ref[...] = jnp.cos(ang)
    sin_ref[...] = jnp.sin(ang) * srow_ref[...]
    angt = fcol_ref[...] * post_ref[0].astype(F32)
    cost_ref[0] = jnp.cos(angt)
    sint_ref[0] = jnp.sin(angt) * scol_ref[...]


def _rope_tables(positions, tm):
    b, s = positions.shape
    t = b * s
    inv_freq = jnp.power(ROPE_THETA, -jnp.arange(0, MLA_ROPE, 2, dtype=F32) / MLA_ROPE)
    half = MLA_ROPE // 2
    zeros = lambda n: jnp.zeros((n,), F32)
    freq = jnp.concatenate([zeros(MLA_NOPE), inv_freq, inv_freq, zeros(HEAD_LANES - MLA_NOPE - MLA_ROPE)])
    sign = jnp.concatenate([zeros(MLA_NOPE), -jnp.ones((half,), F32), jnp.ones((half,), F32),
                            zeros(HEAD_LANES - MLA_NOPE - MLA_ROPE)])
    nsteps = s // tm
    return pl.pallas_call(
        _rope_body,
        out_shape=(jax.ShapeDtypeStruct((t, HEAD_LANES), F32), jax.ShapeDtypeStruct((t, HEAD_LANES), F32),
                   jax.ShapeDtypeStruct((b, HEAD_LANES, s), F32), jax.ShapeDtypeStruct((b, HEAD_LANES, s), F32)),
        grid=(b, nsteps),
        in_specs=[pl.BlockSpec((tm, 1), lambda bi, i: (bi * nsteps + i, 0)),
                  pl.BlockSpec((1, 1, tm), lambda bi, i: (bi, 0, i)),
                  _const_spec((1, HEAD_LANES)), _const_spec((1, HEAD_LANES)),
                  _const_spec((HEAD_LANES, 1)), _const_spec((HEAD_LANES, 1))],
        out_specs=(pl.BlockSpec((tm, HEAD_LANES), lambda bi, i: (bi * nsteps + i, 0)),
                   pl.BlockSpec((tm, HEAD_LANES), lambda bi, i: (bi * nsteps + i, 0)),
                   pl.BlockSpec((1, HEAD_LANES, tm), lambda bi, i: (bi, 0, i)),
                   pl.BlockSpec((1, HEAD_LANES, tm), lambda bi, i: (bi, 0, i))),
        compiler_params=_grid_params(2),
        name="rope_tables",
    )(positions.reshape(t, 1), positions.reshape(b, 1, s), freq[None, :], sign[None, :], freq[:, None], sign[:, None])


GDN_W = GDN_HEADS * GDN_D
SEG_QKV = (0, 3 * GDN_W)
SEG_Z = (SEG_QKV[1], SEG_QKV[1] + GDN_W)
SEG_BA = (SEG_Z[1], SEG_Z[1] + 128)
SEG_CQ = (SEG_BA[1], SEG_BA[1] + MLA_Q_LORA)
SEG_CKV = (SEG_CQ[1], SEG_CQ[1] + MLA_KV_LORA)
SEG_KRM = (SEG_CKV[1], SEG_CKV[1] + HEAD_LANES)
SEG_KRP = (SEG_KRM[1], SEG_KRM[1] + HEAD_LANES)
SEG_GATE = (SEG_KRP[1], SEG_KRP[1] + 2048)
N_BA = 4 * GDN_HEADS


def _inproj_body(x_ref, nw_ref, w_ref, wbat_ref, alog_ref, dtb_ref, alogt_ref, dtbt_ref, qnw_ref, kvnw_ref,
                 cos_ref, sin_ref, qkv_ref, sz_ref, bg_ref, gt_ref, cqn_ref, ckvn_ref, kr_ref, sg_ref):
    hb = _rms(x_ref[...], nw_ref[...]).astype(BF16)

    def seg(bounds):
        return _dot(hb, w_ref[:, bounds[0]:bounds[1]])

    qkv_ref[...] = seg(SEG_QKV)
    sz_ref[...] = _silu(seg(SEG_Z))
    ba = seg(SEG_BA)
    lane = lax.broadcasted_iota(jnp.int32, ba.shape, 1)
    decay = -jnp.exp(alog_ref[...]) * _softplus(ba + dtb_ref[...])
    bg_ref[...] = jnp.where(lane < 2 * GDN_HEADS, jax.nn.sigmoid(ba), decay)[:, :N_BA]
    bat = lax.dot_general(wbat_ref[...], hb, NT_DIMS, preferred_element_type=F32)
    at = bat[2 * GDN_HEADS:, :]
    gt_ref[...] = -jnp.exp(alogt_ref[...]) * _softplus(at + dtbt_ref[...])
    cqn_ref[...] = _rms(seg(SEG_CQ), qnw_ref[...]).astype(BF16)
    ckvn_ref[...] = _rms(seg(SEG_CKV), kvnw_ref[...]).astype(BF16)
    kr_ref[...] = seg(SEG_KRM) * cos_ref[...] + seg(SEG_KRP) * sin_ref[...]
    sg_ref[...] = jax.nn.sigmoid(seg(SEG_GATE))


def _inproj(x, nw, w, wbat, alog, dtb, alogt, dtbt, qnw, kvnw, cos, sin, *, tm):
    t, d = x.shape
    row = lambda n: pl.BlockSpec((tm, n), lambda i: (i, 0))
    out_shape = (
        jax.ShapeDtypeStruct((t, 3 * GDN_W), F32),
        jax.ShapeDtypeStruct((t, GDN_W), F32),
        jax.ShapeDtypeStruct((t, N_BA), F32),
        jax.ShapeDtypeStruct((2 * GDN_HEADS, t), F32),
        jax.ShapeDtypeStruct((t, MLA_Q_LORA), BF16),
        jax.ShapeDtypeStruct((t, MLA_KV_LORA), BF16),
        jax.ShapeDtypeStruct((t, HEAD_LANES), F32),
        jax.ShapeDtypeStruct((t, 2048), F32),
    )
    out_specs = (row(3 * GDN_W), row(GDN_W), row(N_BA), pl.BlockSpec((2 * GDN_HEADS, tm), lambda i: (0, i)),
                 row(MLA_Q_LORA), row(MLA_KV_LORA), row(HEAD_LANES), row(2048))
    return pl.pallas_call(
        _inproj_body,
        out_shape=out_shape,
        grid=(t // tm,),
        in_specs=[row(d), _const_spec((1, d)), _const_spec(w.shape), _const_spec(wbat.shape),
                  _const_spec((1, 128)), _const_spec((1, 128)),
                  _const_spec((2 * GDN_HEADS, 1)), _const_spec((2 * GDN_HEADS, 1)),
                  _const_spec((1, MLA_Q_LORA)), _const_spec((1, MLA_KV_LORA)),
                  row(HEAD_LANES), row(HEAD_LANES)],
        out_specs=out_specs,
        compiler_params=_grid_params(1),
        name="inproj",
    )(x, nw, w, wbat, alog, dtb, alogt, dtbt, qnw, kvnw, cos, sin)


def _mla_prep_body(cqn_ref, ckvn_ref, kr_ref, cost_ref, sint_ref, wqm_ref, wqp_ref, wk_ref, wvt_ref, ones_ref,
                   qt_ref, k_ref, vt_ref):
    cqn = cqn_ref[...]
    qm = lax.dot_general(wqm_ref[...], cqn, NT_DIMS, preferred_element_type=F32)
    qp = lax.dot_general(wqp_ref[...], cqn, NT_DIMS, preferred_element_type=F32)
    qscale = (MLA_NOPE + MLA_ROPE) ** -0.5 * LOG2E
    cost = cost_ref[0] * qscale
    sint = sint_ref[0] * qscale
    ckvn = ckvn_ref[...]
    km = _dot(ckvn, wk_ref[...])
    kr = kr_ref[...]
    for h in range(MLA_HEADS):
        grp = slice(h * HEAD_LANES, (h + 1) * HEAD_LANES)
        qt_ref[0, grp, :] = (qm[grp, :] * cost + qp[grp, :] * sint).astype(BF16)
        k_ref[:, grp] = (km[:, grp] + kr).astype(BF16)
    vt = lax.dot_general(wvt_ref[...], ckvn, NT_DIMS, preferred_element_type=F32)
    vt_ref[0, 0] = (vt + ones_ref[...]).astype(BF16)


def _mla_prep(cqn, ckvn, kr, cost, sint, wqm, wqp, wk, wvt, ones, *, b, s, tk):
    t = b * s
    nsteps = s // tk
    hl = MLA_HEADS * HEAD_LANES
    vr = MLA_HEADS * V_ROWS
    row = lambda n: pl.BlockSpec((tk, n), lambda bi, i: (bi * nsteps + i, 0))
    tr = pl.BlockSpec((1, HEAD_LANES, tk), lambda bi, i: (bi, 0, i))
    return pl.pallas_call(
        _mla_prep_body,
        out_shape=(jax.ShapeDtypeStruct((b, hl, s), BF16), jax.ShapeDtypeStruct((t, hl), BF16),
                   jax.ShapeDtypeStruct((b, nsteps, vr, tk), BF16)),
        grid=(b, nsteps),
        in_specs=[row(MLA_Q_LORA), row(MLA_KV_LORA), row(HEAD_LANES), tr, tr,
                  _const_spec(wqm.shape), _const_spec(wqp.shape), _const_spec(wk.shape), _const_spec(wvt.shape),
                  _const_spec((vr, 1))],
        out_specs=(pl.BlockSpec((1, hl, tk), lambda bi, i: (bi, 0, i)), row(hl),
                   pl.BlockSpec((1, 1, vr, tk), lambda bi, i: (bi, i, 0, 0))),
        compiler_params=_grid_params(2),
        name="mla_prep",
    )(cqn, ckvn, kr, cost, sint, wqm, wqp, wk, wvt, ones)


def _attn_body(qt_ref, k_ref, vt_ref, ot_ref, m_ref, acc_ref, *, tk):
    qt = qt_ref[0]
    nblk = k_ref.shape[0] // tk
    m_ref[...] = jnp.full(m_ref.shape, NEG_BIG, F32)
    acc_ref[...] = jnp.zeros(acc_ref.shape, F32)

    def step(j, carry):
        kblk = k_ref[pl.ds(pl.multiple_of(j * tk, tk), tk), :]
        s = _dot(kblk, qt)
        m_old = m_ref[...]
        m_new = jnp.maximum(m_old, jnp.max(s, axis=0, keepdims=True))
        p = jnp.exp2(s - m_new).astype(BF16)
        acc_ref[...] = acc_ref[...] * jnp.exp2(m_old - m_new) + _dot(vt_ref[0, j], p)
        m_ref[...] = m_new
        return carry

    lax.fori_loop(0, nblk, step, 0)
    acc = acc_ref[...]
    ot_ref[0] = acc[:MLA_V, :] / acc[MLA_V:MLA_V + 1, :]


def _attention(qt, k, vt, *, b, s, tq, tk):
    nq = s // tq
    nk = s // tk
    return pl.pallas_call(
        functools.partial(_attn_body, tk=tk),
        out_shape=jax.ShapeDtypeStruct((b, MLA_HEADS * MLA_V, s), F32),
        grid=(b, MLA_HEADS, nq),
        in_specs=[pl.BlockSpec((1, HEAD_LANES, tq), lambda bi, h, i: (bi, h, i)),
                  pl.BlockSpec((s, HEAD_LANES), lambda bi, h, i: (bi, h)),
                  pl.BlockSpec((1, nk, V_ROWS, tk), lambda bi, h, i: (bi, 0, h, 0))],
        out_specs=pl.BlockSpec((1, MLA_V, tq), lambda bi, h, i: (bi, h, i)),
        scratch_shapes=[pltpu.VMEM((1, tq), F32), pltpu.VMEM((V_ROWS, tq), F32)],
        compiler_params=_grid_params(3),
        name="attention",
    )(qt, k, vt)


HALO = 8


def _gdn_prep_body(prev_ref, cur_ref, next_ref, cw_ref, q_ref, k_ref, v_ref, buf_ref, *, tiles_per_seq):
    i = pl.program_id(0)
    tm = cur_ref.shape[0]
    first = (i % tiles_per_seq) == 0
    last = (i % tiles_per_seq) == tiles_per_seq - 1
    buf_ref[0:HALO, :] = jnp.where(first, 0.0, prev_ref[...])
    buf_ref[HALO:HALO + tm, :] = cur_ref[...]
    buf_ref[HALO + tm:2 * HALO + tm, :] = jnp.where(last, 0.0, next_ref[...])
    pad = GDN_CONV // 2
    for grp in range(3 * GDN_HEADS):
        lanes = slice(grp * GDN_D, (grp + 1) * GDN_D)
        acc = None
        for j in range(GDN_CONV):
            lo = HALO - pad + j
            term = buf_ref[lo:lo + tm, lanes] * cw_ref[j:j + 1, lanes]
            acc = term if acc is None else acc + term
        y = _silu(acc)
        if grp < 2 * GDN_HEADS:
            y = y * lax.rsqrt(jnp.sum(y * y, axis=-1, keepdims=True) + EPS)
        if grp < GDN_HEADS:
            q_ref[:, lanes] = y * GDN_D ** -0.5
        elif grp < 2 * GDN_HEADS:
            k_ref[:, slice(lanes.start - GDN_W, lanes.stop - GDN_W)] = y
        else:
            v_ref[:, slice(lanes.start - 2 * GDN_W, lanes.stop - 2 * GDN_W)] = y


def _gdn_prep(qkv, conv_w, *, s, tm):
    t, c = qkv.shape
    tiles_per_seq = s // tm
    hb = tm // HALO
    nh = t // HALO
    out = jax.ShapeDtypeStruct((t, GDN_W), F32)
    row = pl.BlockSpec((tm, GDN_W), lambda i: (i, 0))
    return pl.pallas_call(
        functools.partial(_gdn_prep_body, tiles_per_seq=tiles_per_seq),
        out_shape=(out, out, out),
        grid=(t // tm,),
        in_specs=[pl.BlockSpec((HALO, c), lambda i: (jnp.maximum(i * hb - 1, 0), 0)),
                  pl.BlockSpec((tm, c), lambda i: (i, 0)),
                  pl.BlockSpec((HALO, c), lambda i: (jnp.minimum((i + 1) * hb, nh - 1), 0)),
                  _const_spec(conv_w.shape)],
        out_specs=(row, row, row),
        scratch_shapes=[pltpu.VMEM((tm + 2 * HALO, c), F32)],
        compiler_params=_grid_params(1),
        name="gdn_prep",
    )(qkv, qkv, qkv, conv_w)


CH = GDN_CHUNK
HC = GDN_HEADS * CH


def _split2(x):
    hi = x.astype(BF16)
    lo = (x - hi.astype(F32)).astype(BF16)
    return hi, lo


def _split3(x):
    hi = x.astype(BF16)
    r = x - hi.astype(F32)
    mid = r.astype(BF16)
    lo = (r - mid.astype(F32)).astype(BF16)
    return hi, mid, lo


def _dot3(a, b):
    ah, al = _split2(a)
    bh, bl = _split2(b)
    return _dot(ah, bh) + _dot(ah, bl) + _dot(al, bh)


def _bcast_cols(cols, width):
    if width == 128:
        return jnp.concatenate([jnp.broadcast_to(c, (CH, 128)) for c in cols], axis=1)
    lane = lax.broadcasted_iota(jnp.int32, (CH, 128), 1)
    lo = lane < 64
    pair = lambda a, b: jnp.where(lo, jnp.broadcast_to(a, (CH, 128)), jnp.broadcast_to(b, (CH, 128)))
    return jnp.concatenate([pair(cols[0], cols[1]), pair(cols[2], cols[3])], axis=1)


def _block_diag(m, bd_mask):
    return jnp.where(bd_mask, jnp.concatenate([m] * GDN_HEADS, axis=0), 0.0)


def _gdn_chunk(q, k, v, bg, gt, d, s_ref, o_ref, rows):
    r64 = lax.broadcasted_iota(jnp.int32, (CH, CH), 0)
    c64 = lax.broadcasted_iota(jnp.int32, (CH, CH), 1)
    r256 = lax.broadcasted_iota(jnp.int32, (CH, HC), 0)
    c256 = lax.broadcasted_iota(jnp.int32, (CH, HC), 1) & (CH - 1)
    if d == 0:
        tri = (c64 <= r64)
        incl = (c256 <= r256)
        strict = (c256 < r256)
        trit = (r256 <= c256)
        last = CH - 1
    else:
        tri = (c64 >= r64)
        incl = (c256 >= r256)
        strict = (c256 > r256)
        trit = (r256 >= c256)
        last = 0
    tri = tri.astype(BF16)
    trit = trit.astype(BF16)

    bh, bm, bl = _split3(bg)
    cs = _dot(tri, bh) + _dot(tri, bm) + _dot(tri, bl)
    gh, gm, gl = _split3(gt)
    cst = _dot(gh, trit) + _dot(gm, trit) + _dot(gl, trit)
    r8 = lax.broadcasted_iota(jnp.int32, (2 * GDN_HEADS, HC), 0)
    h8 = lax.broadcasted_iota(jnp.int32, (2 * GDN_HEADS, HC), 1) >> 6
    crow = jnp.sum(jnp.where(r8 == d * GDN_HEADS + h8, cst, 0.0), axis=0, keepdims=True)
    goff = 2 * GDN_HEADS + d * GDN_HEADS
    ccols = [cs[:, goff + h:goff + h + 1] for h in range(GDN_HEADS)]
    betas = [bg[:, d * GDN_HEADS + h:d * GDN_HEADS + h + 1] for h in range(GDN_HEADS)]
    ccol256 = _bcast_cols(ccols, CH)
    ccol512 = _bcast_cols(ccols, GDN_D)
    beta512 = _bcast_cols(betas, GDN_D)
    decay = jnp.exp(jnp.where(incl, ccol256 - crow, -jnp.inf))
    ecol512 = jnp.exp(ccol512)

    kb = k * beta512
    bd_rows = lax.broadcasted_iota(jnp.int32, (HC, GDN_W), 0) >> 6
    bd_cols = lax.broadcasted_iota(jnp.int32, (HC, GDN_W), 1) >> 7
    kbd = jnp.where(bd_rows == bd_cols, jnp.concatenate([k] * GDN_HEADS, axis=0), 0.0).astype(BF16)
    kq = lax.dot_general(jnp.concatenate([kb, q], axis=0).astype(BF16), kbd, NT_DIMS,
                         preferred_element_type=F32)
    neg_l = jnp.where(strict, -(kq[:CH] * decay), 0.0)
    intra = kq[CH:] * decay

    bd_mask = (lax.broadcasted_iota(jnp.int32, (HC, HC), 0) >> 6) == (lax.broadcasted_iota(jnp.int32, (HC, HC), 1) >> 6)
    eye = (c256 == r256).astype(F32)
    p = eye + neg_l
    lm = _dot3(neg_l, _block_diag(neg_l, bd_mask))
    for it in range(5):
        w_bd = _block_diag(lm, bd_mask)
        if it < 4:
            y = _dot3(jnp.concatenate([p, lm], axis=0), w_bd)
            p = p + y[:CH]
            lm = y[CH:]
        else:
            p = p + _dot3(p, w_bd)
    tinv = p

    vb = v * beta512
    kbe = kb * ecol512
    qe = q * ecol512
    for h in range(GDN_HEADS):
        hl = slice(h * GDN_D, (h + 1) * GDN_D)
        hc = slice(h * CH, (h + 1) * CH)
        rhs = jnp.concatenate([vb[:, hl], kbe[:, hl]], axis=1).astype(BF16)
        uw = _dot(tinv[:, hc].astype(BF16), rhs)
        u, w = uw[:, :GDN_D], uw[:, GDN_D:]
        state = s_ref[d * GDN_HEADS + h]
        ws = _dot(jnp.concatenate([w, qe[:, hl]], axis=0).astype(BF16), state.astype(BF16))
        vnew = u - ws[:CH]
        vnb = vnew.astype(BF16)
        o_ref[rows, hl] = ws[CH:] + _dot(intra[:, hc].astype(BF16), vnb)
        glast = ccols[h][last:last + 1, :]
        kd = k[:, hl] * jnp.exp(glast - ccol512[:, hl])
        s_ref[d * GDN_HEADS + h] = state * jnp.exp(glast) + _dot(kd.T.astype(BF16), vnb)


def _gdn_scan_body(qf_ref, kf_ref, vf_ref, bgf_ref, gtf_ref, qb_ref, kb_ref, vb_ref, bgb_ref, gtb_ref,
                   of_ref, ob_ref, s_ref, *, chunks):
    @pl.when(pl.program_id(1) == 0)
    def _():
        s_ref[...] = jnp.zeros(s_ref.shape, F32)

    for c in range(chunks):
        rows = slice(c * CH, (c + 1) * CH)
        _gdn_chunk(qf_ref[rows, :], kf_ref[rows, :], vf_ref[rows, :], bgf_ref[rows, :], gtf_ref[:, rows],
                   0, s_ref, of_ref, rows)
    for c in reversed(range(chunks)):
        rows = slice(c * CH, (c + 1) * CH)
        _gdn_chunk(qb_ref[rows, :], kb_ref[rows, :], vb_ref[rows, :], bgb_ref[rows, :], gtb_ref[:, rows],
                   1, s_ref, ob_ref, rows)


def _gdn_scan(q, k, v, bg, gt, *, b, s, rows):
    t = b * s
    nsteps = s // rows
    fwd = lambda bi, i: (bi * nsteps + i, 0)
    bwd = lambda bi, i: (bi * nsteps + nsteps - 1 - i, 0)
    fwd_t = lambda bi, i: (0, bi * nsteps + i)
    bwd_t = lambda bi, i: (0, bi * nsteps + nsteps - 1 - i)
    wide = lambda im: pl.BlockSpec((rows, GDN_W), im)
    specs = lambda im, imt: [wide(im), wide(im), wide(im), pl.BlockSpec((rows, N_BA), im),
                             pl.BlockSpec((2 * GDN_HEADS, rows), imt)]
    out = jax.ShapeDtypeStruct((t, GDN_W), F32)
    return pl.pallas_call(
        functools.partial(_gdn_scan_body, chunks=rows // CH),
        out_shape=(out, out),
        grid=(b, nsteps),
        in_specs=specs(fwd, fwd_t) + specs(bwd, bwd_t),
        out_specs=(wide(fwd), wide(bwd)),
        scratch_shapes=[pltpu.VMEM((2 * GDN_HEADS, GDN_D, GDN_D), F32)],
        compiler_params=_grid_params(2),
        name="gdn_scan",
    )(q, k, v, bg, gt, q, k, v, bg, gt)


def _merge_body(x_ref, of_ref, ob_ref, sz_ref, sg_ref, ot_ref, gnw_ref, wgp_ref, wmp_ref, wo_ref, o_ref):
    o = of_ref[...] + ob_ref[...]
    gnw = gnw_ref[...]
    heads = []
    for h in range(GDN_HEADS):
        oh = o[:, h * GDN_D:(h + 1) * GDN_D]
        heads.append(_rms(oh, gnw))
    on = jnp.concatenate(heads, axis=1) * sz_ref[...]
    ya = _dot(on.astype(BF16), wgp_ref[...])
    yb = _dot(ot_ref[0].T.astype(BF16), wmp_ref[...])
    d = ya.shape[1]
    y = sg_ref[:, :d] * ya + sg_ref[:, d:] * yb
    o_ref[...] = x_ref[...] + _dot(y.astype(BF16), wo_ref[...])


def _merge(x, of, ob, sz, sg, ot, gnw, wgp, wmp, wo, *, b, s, tm):
    t, d = x.shape
    nsteps = s // tm
    row = lambda n: pl.BlockSpec((tm, n), lambda bi, i: (bi * nsteps + i, 0))
    return pl.pallas_call(
        _merge_body,
        out_shape=jax.ShapeDtypeStruct((t, d), F32),
        grid=(b, nsteps),
        in_specs=[row(d), row(GDN_W), row(GDN_W), row(GDN_W), row(2 * d),
                  pl.BlockSpec((1, MLA_HEADS * MLA_V, tm), lambda bi, i: (bi, 0, i)),
                  _const_spec((1, GDN_D)), _const_spec(wgp.shape), _const_spec(wmp.shape), _const_spec(wo.shape)],
        out_specs=row(d),
        compiler_params=_grid_params(2),
        name="merge",
    )(x, of, ob, sz, sg, ot, gnw, wgp, wmp, wo)


def _pack_w_in(w):
    d = w.shape[0]
    zeros = lambda n: jnp.zeros((d, n), w.dtype)
    o = 4 * GDN_W
    ba = w[:, o:o + N_BA]
    o += N_BA
    cq = w[:, o:o + MLA_Q_LORA]
    o += MLA_Q_LORA
    ckv = w[:, o:o + MLA_KV_LORA]
    o += MLA_KV_LORA
    kr = w[:, o:o + MLA_ROPE]
    o += MLA_ROPE
    gates = w[:, o:]
    half = MLA_ROPE // 2
    tail = zeros(HEAD_LANES - MLA_NOPE - MLA_ROPE)
    kr_main = jnp.concatenate([zeros(MLA_NOPE), kr, tail], axis=1)
    kr_swap = jnp.concatenate([zeros(MLA_NOPE), kr[:, half:], kr[:, :half], tail], axis=1)
    packed = jnp.concatenate([w[:, :4 * GDN_W], ba, zeros(128 - N_BA), cq, ckv, kr_main, kr_swap, gates], axis=1)
    return packed.astype(BF16), ba.T.astype(BF16)


def _pack_w_uq(w):
    r = w.shape[0]
    qk = MLA_NOPE + MLA_ROPE
    half = MLA_ROPE // 2
    tail = jnp.zeros((r, HEAD_LANES - qk), w.dtype)
    znope = jnp.zeros((r, MLA_NOPE), w.dtype)
    main, swap = [], []
    for h in range(MLA_HEADS):
        nope = w[:, h * qk:h * qk + MLA_NOPE]
        rope = w[:, h * qk + MLA_NOPE:(h + 1) * qk]
        main += [nope, rope, tail]
        swap += [znope, rope[:, half:], rope[:, :half], tail]
    return jnp.concatenate(main, axis=1).T.astype(BF16), jnp.concatenate(swap, axis=1).T.astype(BF16)


def _pack_w_ukv(w):
    r = w.shape[0]
    hw = MLA_NOPE + MLA_V
    ks, vs = [], []
    for h in range(MLA_HEADS):
        ks += [w[:, h * hw:h * hw + MLA_NOPE], jnp.zeros((r, HEAD_LANES - MLA_NOPE), w.dtype)]
        vs += [w[:, h * hw + MLA_NOPE:(h + 1) * hw], jnp.zeros((r, V_ROWS - MLA_V), w.dtype)]
    return jnp.concatenate(ks, axis=1).astype(BF16), jnp.concatenate(vs, axis=1).T.astype(BF16)


def _ones_rows():
    idx = jnp.arange(MLA_HEADS * V_ROWS) % V_ROWS
    return (idx == MLA_V).astype(F32)[:, None]


def _lane_pad(v, lo, width):
    return jnp.zeros((1, width), v.dtype).at[0, lo:lo + v.shape[0]].set(v)


def _tile_rows(t):
    return 512 if t % 512 == 0 else 256


def kernel(x, positions, norm_ffn1, ffn1_w_gate, ffn1_w_up, ffn1_w_down, norm_mix, w_in, gdn_conv, gdn_A_log,
           gdn_dt_bias, gdn_norm, gdn_proj, mla_q_norm, mla_w_uq, mla_kv_norm, mla_w_ukv, mla_proj, w_out,
           norm_ffn2, ffn2_w_gate, ffn2_w_up, ffn2_w_down, final_norm):
    b, s, d = x.shape
    t = b * s
    depth = w_in.shape[0]
    tm = _tile_rows(s)
    ffn_tm = 256
    tq = tk = tm
    scan_rows = 2 * CH

    cos, sin, cost, sint = _rope_tables(positions, tm)
    ones = _ones_rows()
    xf = x.reshape(t, d)
    row = lambda v: v.reshape(1, -1)
    for l in range(depth):
        xf = _ffn(xf, row(norm_ffn1[l]), ffn1_w_gate[l].astype(BF16), ffn1_w_up[l].astype(BF16),
                  ffn1_w_down[l].astype(BF16), row(final_norm), final_norm=False, tm=ffn_tm)

        w_packed, w_bat = _pack_w_in(w_in[l])
        alog = gdn_A_log[l].reshape(-1)
        dtb = gdn_dt_bias[l].reshape(-1)
        qkv, sz, bg, gt, cqn, ckvn, kr, sg = _inproj(
            xf, row(norm_mix[l]), w_packed, w_bat, _lane_pad(alog, 2 * GDN_HEADS, 128),
            _lane_pad(dtb, 2 * GDN_HEADS, 128), alog[:, None], dtb[:, None], row(mla_q_norm[l]),
            row(mla_kv_norm[l]), cos, sin, tm=tm)

        qn, kn, vv = _gdn_prep(qkv, gdn_conv[l], s=s, tm=tm)
        of, ob = _gdn_scan(qn, kn, vv, bg, gt, b=b, s=s, rows=scan_rows)

        wqm, wqp = _pack_w_uq(mla_w_uq[l])
        wk, wvt = _pack_w_ukv(mla_w_ukv[l])
        qt, kk, vt = _mla_prep(cqn, ckvn, kr, cost, sint, wqm, wqp, wk, wvt, ones, b=b, s=s, tk=tk)
        ot = _attention(qt, kk, vt, b=b, s=s, tq=tq, tk=tk)

        xf = _merge(xf, of, ob, sz, sg, ot, row(gdn_norm[l]), gdn_proj[l].astype(BF16), mla_proj[l].astype(BF16),
                    w_out[l].astype(BF16), b=b, s=s, tm=tm)

        xf = _ffn(xf, row(norm_ffn2[l]), ffn2_w_gate[l].astype(BF16), ffn2_w_up[l].astype(BF16),
                  ffn2_w_down[l].astype(BF16), row(final_norm), final_norm=(l == depth - 1), tm=ffn_tm)
    return xf.reshape(b, s, d)
```

```python
import functools

import jax
import jax.numpy as jnp
from jax import lax
from jax.experimental import pallas as pl
from jax.experimental.pallas import tpu as pltpu

F32 = jnp.float32
BF16 = jnp.bfloat16

EPS = 1e-6
RES_HALF = 0.5
GDN_HEADS = 4
GDN_D = 128
GDN_CONV = 5
GDN_CHUNK = 64
MLA_HEADS = 8
MLA_NOPE = 64
MLA_ROPE = 32
MLA_V = 64
MLA_Q_LORA = 384
MLA_KV_LORA = 256
ROPE_THETA = 10000.0
HEAD_LANES = 128
V_ROWS = 80
LOG2E = 1.4426950408889634
NEG_BIG = -1e30

VMEM_LIMIT_BYTES = 56 * 1024 * 1024

NT_DIMS = (((1,), (1,)), ((), ()))


def _grid_params(n, flags=None):
    return pltpu.CompilerParams(dimension_semantics=("arbitrary",) * n, vmem_limit_bytes=VMEM_LIMIT_BYTES,
                                flags=flags)


def _const_spec(shape):
    nd = len(shape)
    return pl.BlockSpec(shape, lambda *_: (0,) * nd, pipeline_mode=pl.Buffered(1))


def _rms(x, w):
    return x * lax.rsqrt(jnp.mean(x * x, axis=-1, keepdims=True) + EPS) * w


def _silu(x):
    return x * jax.nn.sigmoid(x)


def _softplus(x):
    return jnp.maximum(x, 0.0) + jnp.log1p(jnp.exp(-jnp.abs(x)))


def _dot(a, b):
    return jnp.dot(a, b, preferred_element_type=F32)


def _ffn_body(x_ref, nw_ref, wg_ref, wu_ref, wd_ref, fw_ref, o_ref, *, final_norm):
    x = x_ref[...]
    hb = _rms(x, nw_ref[...]).astype(BF16)
    g = _dot(hb, wg_ref[...])
    u = _dot(hb, wu_ref[...])
    a = (_silu(g) * u).astype(BF16)
    y = x + RES_HALF * _dot(a, wd_ref[...])
    if final_norm:
        y = _rms(y, fw_ref[...])
    o_ref[...] = y


def _ffn(x, nw, wg, wu, wd, fw, *, final_norm, tm):
    t, d = x.shape
    ff = wg.shape[1]
    row = pl.BlockSpec((tm, d), lambda i: (i, 0))
    return pl.pallas_call(
        functools.partial(_ffn_body, final_norm=final_norm),
        out_shape=jax.ShapeDtypeStruct((t, d), F32),
        grid=(t // tm,),
        in_specs=[row, _const_spec((1, d)), _const_spec((d, ff)), _const_spec((d, ff)),
                  _const_spec((ff, d)), _const_spec((1, d))],
        out_specs=row,
        compiler_params=_grid_params(1),
        name="ffn",
    )(x, nw, wg, wu, wd, fw)


def _rope_body(pos_ref, post_ref, frow_ref, srow_ref, fcol_ref, scol_ref, cos_ref, sin_ref, cost_ref, sint_ref):
    ang = pos_ref[...].astype(F32) * frow_ref[...]
    cos_ref[...] = jnp.cos(ang)
    sin_ref[...] = jnp.sin(ang) * srow_ref[...]
    angt = fcol_ref[...] * post_ref[0].astype(F32)
    cost_ref[0] = jnp.cos(angt)
    sint_ref[0] = jnp.sin(angt) * scol_ref[...]


def _rope_tables(positions, tm):
    b, s = positions.shape
    t = b * s
    inv_freq = jnp.power(ROPE_THETA, -jnp.arange(0, MLA_ROPE, 2, dtype=F32) / MLA_ROPE)
    half = MLA_ROPE // 2
    zeros = lambda n: jnp.zeros((n,), F32)
    freq = jnp.concatenate([zeros(MLA_NOPE), inv_freq, inv_freq, zeros(HEAD_LANES - MLA_NOPE - MLA_ROPE)])
    sign = jnp.concatenate([zeros(MLA_NOPE), -jnp.ones((half,), F32), jnp.ones((half,), F32),
                            zeros(HEAD_LANES - MLA_NOPE - MLA_ROPE)])
    nsteps = s // tm
    return pl.pallas_call(
        _rope_body,
        out_shape=(jax.ShapeDtypeStruct((t, HEAD_LANES), F32), jax.ShapeDtypeStruct((t, HEAD_LANES), F32),
                   jax.ShapeDtypeStruct((b, HEAD_LANES, s), F32), jax.ShapeDtypeStruct((b, HEAD_LANES, s), F32)),
        grid=(b, nsteps),
        in_specs=[pl.BlockSpec((tm, 1), lambda bi, i: (bi * nsteps + i, 0)),
                  pl.BlockSpec((1, 1, tm), lambda bi, i: (bi, 0, i)),
                  _const_spec((1, HEAD_LANES)), _const_spec((1, HEAD_LANES)),
                  _const_spec((HEAD_LANES, 1)), _const_spec((HEAD_LANES, 1))],
        out_specs=(pl.BlockSpec((tm, HEAD_LANES), lambda bi, i: (bi * nsteps + i, 0)),
                   pl.BlockSpec((tm, HEAD_LANES), lambda bi, i: (bi * nsteps + i, 0)),
                   pl.BlockSpec((1, HEAD_LANES, tm), lambda bi, i: (bi, 0, i)),
                   pl.BlockSpec((1, HEAD_LANES, tm), lambda bi, i: (bi, 0, i))),
        compiler_params=_grid_params(2),
        name="rope_tables",
    )(positions.reshape(t, 1), positions.reshape(b, 1, s), freq[None, :], sign[None, :], freq[:, None], sign[:, None])


GDN_W = GDN_HEADS * GDN_D
SEG_QKV = (0, 3 * GDN_W)
SEG_Z = (SEG_QKV[1], SEG_QKV[1] + GDN_W)
SEG_BA = (SEG_Z[1], SEG_Z[1] + 128)
SEG_CQ = (SEG_BA[1], SEG_BA[1] + MLA_Q_LORA)
SEG_CKV = (SEG_CQ[1], SEG_CQ[1] + MLA_KV_LORA)
SEG_KRM = (SEG_CKV[1], SEG_CKV[1] + HEAD_LANES)
SEG_KRP = (SEG_KRM[1], SEG_KRM[1] + HEAD_LANES)
SEG_GATE = (SEG_KRP[1], SEG_KRP[1] + 2048)
N_BA = 4 * GDN_HEADS


def _inproj_body(x_ref, nw_ref, w_ref, wbat_ref, alog_ref, dtb_ref, alogt_ref, dtbt_ref, qnw_ref, kvnw_ref,
                 cos_ref, sin_ref, qkv_ref, sz_ref, bg_ref, gt_ref, cqn_ref, ckvn_ref, kr_ref, sg_ref):
    hb = _rms(x_ref[...], nw_ref[...]).astype(BF16)

    def seg(bounds):
        return _dot(hb, w_ref[:, bounds[0]:bounds[1]])

    qkv_ref[...] = seg(SEG_QKV)
    sz_ref[...] = _silu(seg(SEG_Z))
    ba = seg(SEG_BA)
    lane = lax.broadcasted_iota(jnp.int32, ba.shape, 1)
    decay = -jnp.exp(alog_ref[...]) * _softplus(ba + dtb_ref[...])
    bg_ref[...] = jnp.where(lane < 2 * GDN_HEADS, jax.nn.sigmoid(ba), decay)[:, :N_BA]
    bat = lax.dot_general(wbat_ref[...], hb, NT_DIMS, preferred_element_type=F32)
    at = bat[2 * GDN_HEADS:, :]
    gt_ref[...] = -jnp.exp(alogt_ref[...]) * _softplus(at + dtbt_ref[...])
    cqn_ref[...] = _rms(seg(SEG_CQ), qnw_ref[...]).astype(BF16)
    ckvn_ref[...] = _rms(seg(SEG_CKV), kvnw_ref[...]).astype(BF16)
    kr_ref[...] = seg(SEG_KRM) * cos_ref[...] + seg(SEG_KRP) * sin_ref[...]
    sg_ref[...] = jax.nn.sigmoid(seg(SEG_GATE))


def _inproj(x, nw, w, wbat, alog, dtb, alogt, dtbt, qnw, kvnw, cos, sin, *, tm):
    t, d = x.shape
    row = lambda n: pl.BlockSpec((tm, n), lambda i: (i, 0))
    out_shape = (
        jax.ShapeDtypeStruct((t, 3 * GDN_W), F32),
        jax.ShapeDtypeStruct((t, GDN_W), F32),
        jax.ShapeDtypeStruct((t, N_BA), F32),
        jax.ShapeDtypeStruct((2 * GDN_HEADS, t), F32),
        jax.ShapeDtypeStruct((t, MLA_Q_LORA), BF16),
        jax.ShapeDtypeStruct((t, MLA_KV_LORA), BF16),
        jax.ShapeDtypeStruct((t, HEAD_LANES), F32),
        jax.ShapeDtypeStruct((t, 2048), F32),
    )
    out_specs = (row(3 * GDN_W), row(GDN_W), row(N_BA), pl.BlockSpec((2 * GDN_HEADS, tm), lambda i: (0, i)),
                 row(MLA_Q_LORA), row(MLA_KV_LORA), row(HEAD_LANES), row(2048))
    return pl.pallas_call(
        _inproj_body,
        out_shape=out_shape,
        grid=(t // tm,),
        in_specs=[row(d), _const_spec((1, d)), _const_spec(w.shape), _const_spec(wbat.shape),
                  _const_spec((1, 128)), _const_spec((1, 128)),
                  _const_spec((2 * GDN_HEADS, 1)), _const_spec((2 * GDN_HEADS, 1)),
                  _const_spec((1, MLA_Q_LORA)), _const_spec((1, MLA_KV_LORA)),
                  row(HEAD_LANES), row(HEAD_LANES)],
        out_specs=out_specs,
        compiler_params=_grid_params(1),
        name="inproj",
    )(x, nw, w, wbat, alog, dtb, alogt, dtbt, qnw, kvnw, cos, sin)


def _mla_prep_body(cqn_ref, ckvn_ref, kr_ref, cost_ref, sint_ref, wqm_ref, wqp_ref, wk_ref, wvt_ref, ones_ref,
                   qt_ref, k_ref, vt_ref):
    cqn = cqn_ref[...]
    qm = lax.dot_general(wqm_ref[...], cqn, NT_DIMS, preferred_element_type=F32)
    qp = lax.dot_general(wqp_ref[...], cqn, NT_DIMS, preferred_element_type=F32)
    qscale = (MLA_NOPE + MLA_ROPE) ** -0.5 * LOG2E
    cost = cost_ref[0] * qscale
    sint = sint_ref[0] * qscale
    ckvn = ckvn_ref[...]
    km = _dot(ckvn, wk_ref[...])
    kr = kr_ref[...]
    for h in range(MLA_HEADS):
        grp = slice(h * HEAD_LANES, (h + 1) * HEAD_LANES)
        qt_ref[0, h, 0] = (qm[grp, :] * cost + qp[grp, :] * sint).astype(BF16)
        k_ref[:, grp] = (km[:, grp] + kr).astype(BF16)
    vt = lax.dot_general(wvt_ref[...], ckvn, NT_DIMS, preferred_element_type=F32)
    vt_ref[0, 0] = (vt + ones_ref[...]).astype(BF16)


def _mla_prep(cqn, ckvn, kr, cost, sint, wqm, wqp, wk, wvt, ones, *, b, s, tk):
    t = b * s
    nsteps = s // tk
    hl = MLA_HEADS * HEAD_LANES
    vr = MLA_HEADS * V_ROWS
    row = lambda n: pl.BlockSpec((tk, n), lambda bi, i: (bi * nsteps + i, 0))
    tr = pl.BlockSpec((1, HEAD_LANES, tk), lambda bi, i: (bi, 0, i))
    return pl.pallas_call(
        _mla_prep_body,
        out_shape=(jax.ShapeDtypeStruct((b, MLA_HEADS, nsteps, HEAD_LANES, tk), BF16),
                   jax.ShapeDtypeStruct((t, hl), BF16),
                   jax.ShapeDtypeStruct((b, nsteps, vr, tk), BF16)),
        grid=(b, nsteps),
        in_specs=[row(MLA_Q_LORA), row(MLA_KV_LORA), row(HEAD_LANES), tr, tr,
                  _const_spec(wqm.shape), _const_spec(wqp.shape), _const_spec(wk.shape), _const_spec(wvt.shape),
                  _const_spec((vr, 1))],
        out_specs=(pl.BlockSpec((1, MLA_HEADS, 1, HEAD_LANES, tk), lambda bi, i: (bi, 0, i, 0, 0)), row(hl),
                   pl.BlockSpec((1, 1, vr, tk), lambda bi, i: (bi, i, 0, 0))),
        compiler_params=_grid_params(2),
        name="mla_prep",
    )(cqn, ckvn, kr, cost, sint, wqm, wqp, wk, wvt, ones)


STEPS_PER_ITER = 3
BLOCKS_PER_STEP = 1
ATTN_SLOTS = BLOCKS_PER_STEP * STEPS_PER_ITER


def _attn_body(qt_ref, k_ref, vt_ref, ot_ref, s_buf, p_buf, bm_buf, a_buf, m_ref, acc_ref, *, nblk):
    nq = qt_ref.shape[2]
    tk = p_buf.shape[1]
    nsteps = nq * nblk // BLOCKS_PER_STEP
    m_ref[...] = jnp.full(m_ref.shape, NEG_BIG, F32)
    acc_ref[...] = jnp.zeros(acc_ref.shape, F32)

    def scores(n, slot):
        j = n % nblk
        start = j * tk if isinstance(j, int) else pl.multiple_of(j * tk, tk)
        kblk = k_ref[pl.ds(start, tk), :]
        s = _dot(kblk, qt_ref[0, 0, n // nblk])
        s_buf[slot] = s
        bm_buf[slot] = jnp.max(s, axis=0, keepdims=True)

    def softmax(n, slot):
        m_old = jnp.where(n % nblk == 0, NEG_BIG, m_ref[...])
        m_new = jnp.maximum(m_old, bm_buf[slot])
        a_buf[slot] = jnp.exp2(m_old - m_new)
        p_buf[slot] = jnp.exp2(s_buf[slot] - m_new).astype(BF16)
        m_ref[...] = m_new

    def weighted_values(n, slot):
        acc = acc_ref[...]
        for i in range(BLOCKS_PER_STEP):
            acc = acc * a_buf[slot + i] + _dot(vt_ref[0, (n + i) % nblk], p_buf[slot + i])
        acc_ref[...] = acc
        return n // nblk, acc[:MLA_V, :] / acc[MLA_V:MLA_V + 1, :]

    def step(t, phase, do_pv=True, do_softmax=True, do_scores=True):
        bps = BLOCKS_PER_STEP
        out = weighted_values(bps * (t - 2), bps * ((phase + 1) % 3)) if do_pv else None
        if do_softmax:
            for i in range(bps):
                softmax(bps * (t - 1) + i, bps * ((phase + 2) % 3) + i)
        if do_scores:
            for i in range(bps):
                scores(bps * t + i, bps * phase + i)
        return out

    def write(out):
        ot_ref[0, 0, out[0]] = out[1]

    step(0, 0, do_pv=False, do_softmax=False)
    step(1, 1, do_pv=False)
    nloop = (nsteps - 2) // STEPS_PER_ITER

    def body(u, carry):
        t0 = 2 + STEPS_PER_ITER * u
        outs = [step(t0 + i, (2 + i) % 3) for i in range(STEPS_PER_ITER)]
        for out in outs:
            write(out)
        return carry

    lax.fori_loop(0, nloop, body, 0)
    for t in range(2 + STEPS_PER_ITER * nloop, nsteps + 2):
        write(step(t, t % 3, do_softmax=t <= nsteps, do_scores=t < nsteps))


def _attention(qt, k, vt, *, b, s, tq, tk):
    nq = s // tq
    nk = s // tk
    assert nk % 2 == 0 and nq * nk >= 4
    return pl.pallas_call(
        functools.partial(_attn_body, nblk=nk),
        out_shape=jax.ShapeDtypeStruct((b, MLA_HEADS, nq, MLA_V, tq), F32),
        grid=(b, MLA_HEADS),
        in_specs=[pl.BlockSpec((1, 1, nq, HEAD_LANES, tq), lambda bi, h: (bi, h, 0, 0, 0)),
                  pl.BlockSpec((s, HEAD_LANES), lambda bi, h: (bi, h)),
                  pl.BlockSpec((1, nk, V_ROWS, tk), lambda bi, h: (bi, 0, h, 0))],
        out_specs=pl.BlockSpec((1, 1, nq, MLA_V, tq), lambda bi, h: (bi, h, 0, 0, 0)),
        scratch_shapes=[pltpu.VMEM((ATTN_SLOTS, tk, tq), F32), pltpu.VMEM((ATTN_SLOTS, tk, tq), BF16),
                        pltpu.VMEM((ATTN_SLOTS, 1, tq), F32), pltpu.VMEM((ATTN_SLOTS, 1, tq), F32),
                        pltpu.VMEM((1, tq), F32), pltpu.VMEM((V_ROWS, tq), F32)],
        compiler_params=_grid_params(2),
        name="attention",
    )(qt, k, vt)


HALO = 8


def _gdn_prep_body(prev_ref, cur_ref, next_ref, cw_ref, q_ref, k_ref, v_ref, buf_ref, *, tiles_per_seq):
    i = pl.program_id(0)
    tm = cur_ref.shape[0]
    first = (i % tiles_per_seq) == 0
    last = (i % tiles_per_seq) == tiles_per_seq - 1
    buf_ref[0:HALO, :] = jnp.where(first, 0.0, prev_ref[...])
    buf_ref[HALO:HALO + tm, :] = cur_ref[...]
    buf_ref[HALO + tm:2 * HALO + tm, :] = jnp.where(last, 0.0, next_ref[...])
    pad = GDN_CONV // 2
    for grp in range(3 * GDN_HEADS):
        lanes = slice(grp * GDN_D, (grp + 1) * GDN_D)
        acc = None
        for j in range(GDN_CONV):
            lo = HALO - pad + j
            term = buf_ref[lo:lo + tm, lanes] * cw_ref[j:j + 1, lanes]
            acc = term if acc is None else acc + term
        y = _silu(acc)
        if grp < 2 * GDN_HEADS:
            y = y * lax.rsqrt(jnp.sum(y * y, axis=-1, keepdims=True) + EPS)
        if grp < GDN_HEADS:
            q_ref[:, lanes] = y * GDN_D ** -0.5
        elif grp < 2 * GDN_HEADS:
            k_ref[:, slice(lanes.start - GDN_W, lanes.stop - GDN_W)] = y
        else:
            v_ref[:, slice(lanes.start - 2 * GDN_W, lanes.stop - 2 * GDN_W)] = y


def _gdn_prep(qkv, conv_w, *, s, tm):
    t, c = qkv.shape
    tiles_per_seq = s // tm
    hb = tm // HALO
    nh = t // HALO
    out = jax.ShapeDtypeStruct((t, GDN_W), F32)
    row = pl.BlockSpec((tm, GDN_W), lambda i: (i, 0))
    return pl.pallas_call(
        functools.partial(_gdn_prep_body, tiles_per_seq=tiles_per_seq),
        out_shape=(out, out, out),
        grid=(t // tm,),
        in_specs=[pl.BlockSpec((HALO, c), lambda i: (jnp.maximum(i * hb - 1, 0), 0)),
                  pl.BlockSpec((tm, c), lambda i: (i, 0)),
                  pl.BlockSpec((HALO, c), lambda i: (jnp.minimum((i + 1) * hb, nh - 1), 0)),
                  _const_spec(conv_w.shape)],
        out_specs=(row, row, row),
        scratch_shapes=[pltpu.VMEM((tm + 2 * HALO, c), F32)],
        compiler_params=_grid_params(1),
        name="gdn_prep",
    )(qkv, qkv, qkv, conv_w)


CH = GDN_CHUNK
HC = GDN_HEADS * CH


def _split3(x):
    hi = x.astype(BF16)
    r = x - hi.astype(F32)
    mid = r.astype(BF16)
    lo = (r - mid.astype(F32)).astype(BF16)
    return hi, mid, lo


def _bcast_cols(cols, width):
    if width == 128:
        return jnp.concatenate([jnp.broadcast_to(c, (CH, 128)) for c in cols], axis=1)
    lane = lax.broadcasted_iota(jnp.int32, (CH, 128), 1)
    lo = lane < 64
    pair = lambda a, b: jnp.where(lo, jnp.broadcast_to(a, (CH, 128)), jnp.broadcast_to(b, (CH, 128)))
    return jnp.concatenate([pair(cols[0], cols[1]), pair(cols[2], cols[3])], axis=1)


def _dir_masks(d):
    r64 = lax.broadcasted_iota(jnp.int32, (CH, CH), 0)
    c64 = lax.broadcasted_iota(jnp.int32, (CH, CH), 1)
    r256 = lax.broadcasted_iota(jnp.int32, (CH, HC), 0)
    c256 = lax.broadcasted_iota(jnp.int32, (CH, HC), 1) & (CH - 1)
    if d == 0:
        tri, incl, strict, trit, last = (c64 <= r64), (c256 <= r256), (c256 < r256), (r256 <= c256), CH - 1
    else:
        tri, incl, strict, trit, last = (c64 >= r64), (c256 >= r256), (c256 > r256), (r256 >= c256), 0
    return dict(tri=tri.astype(BF16), trit=trit.astype(BF16), incl=incl, strict=strict, last=last)


def _pair_diag(a, b):
    z = jnp.zeros(a.shape, a.dtype)
    return jnp.concatenate([jnp.concatenate([a, z], axis=1), jnp.concatenate([z, b], axis=1)], axis=0)


def _gdn_scan_body(qf_ref, kf_ref, vf_ref, bgf_ref, gtf_ref, qb_ref, kb_ref, vb_ref, bgb_ref, gtb_ref,
                   of_ref, ob_ref, s_ref):
    @pl.when(pl.program_id(0) == 0)
    def _():
        s_ref[...] = jnp.zeros(s_ref.shape, F32)

    nb = qf_ref.shape[0]
    in_refs = ((qf_ref, kf_ref, vf_ref, bgf_ref, gtf_ref), (qb_ref, kb_ref, vb_ref, bgb_ref, gtb_ref))
    out_refs = (of_ref, ob_ref)
    masks = (_dir_masks(0), _dir_masks(1))
    groups = [(d, b) for b in range(nb) for d in (0, 1)]
    heads = range(GDN_HEADS)
    pairs = range(GDN_HEADS // 2)

    r256 = lax.broadcasted_iota(jnp.int32, (CH, HC), 0)
    c256 = lax.broadcasted_iota(jnp.int32, (CH, HC), 1) & (CH - 1)
    eye = (c256 == r256).astype(F32)
    bd_mask = ((lax.broadcasted_iota(jnp.int32, (HC, HC), 0) >> 6)
               == (lax.broadcasted_iota(jnp.int32, (HC, HC), 1) >> 6)).astype(BF16)
    kbd_mask = ((lax.broadcasted_iota(jnp.int32, (HC, GDN_W), 0) >> 6)
                == (lax.broadcasted_iota(jnp.int32, (HC, GDN_W), 1) >> 7)).astype(BF16)
    r8 = lax.broadcasted_iota(jnp.int32, (2 * GDN_HEADS, HC), 0)
    h8 = lax.broadcasted_iota(jnp.int32, (2 * GDN_HEADS, HC), 1) >> 6

    st = []
    for d, b in groups:
        mk = masks[d]
        bg = in_refs[d][3][b]
        gt = in_refs[d][4][b, 0]
        bh, bm, bl = _split3(bg)
        cs = _dot(mk["tri"], bh) + _dot(mk["tri"], bm) + _dot(mk["tri"], bl)
        gh, gm, gl = _split3(gt)
        cst = _dot(gh, mk["trit"]) + _dot(gm, mk["trit"]) + _dot(gl, mk["trit"])
        st.append(dict(bg=bg, cs=cs, cst=cst))

    for g, (d, b) in enumerate(groups):
        mk, e = masks[d], st[g]
        crow = jnp.sum(jnp.where(r8 == d * GDN_HEADS + h8, e["cst"], 0.0), axis=0, keepdims=True)
        goff = 2 * GDN_HEADS + d * GDN_HEADS
        ccols = [e["cs"][:, goff + h:goff + h + 1] for h in heads]
        betas = [e["bg"][:, d * GDN_HEADS + h:d * GDN_HEADS + h + 1] for h in heads]
        e["ccols"] = ccols
        e["ccol512"] = _bcast_cols(ccols, GDN_D)
        e["beta512"] = _bcast_cols(betas, GDN_D)
        e["decay"] = jnp.exp(jnp.where(mk["incl"], _bcast_cols(ccols, CH) - crow, -jnp.inf))

    for g, (d, b) in enumerate(groups):
        mk, e = masks[d], st[g]
        q = in_refs[d][0][b]
        k = in_refs[d][1][b]
        kb = k * e["beta512"]
        kbd = jnp.concatenate([k.astype(BF16)] * GDN_HEADS, axis=0) * kbd_mask
        kq = lax.dot_general(jnp.concatenate([kb, q], axis=0).astype(BF16), kbd, NT_DIMS,
                             preferred_element_type=F32)
        neg_l = jnp.where(mk["strict"], -(kq[:CH] * e["decay"]), 0.0)
        e["intra"] = (kq[CH:] * e["decay"]).astype(BF16)
        e["p"] = eye + neg_l
        e["lm"] = neg_l

    for level in range(6):
        for e in st:
            w_bd = jnp.concatenate([e["lm"].astype(BF16)] * GDN_HEADS, axis=0) * bd_mask
            if level == 0:
                e["lm"] = _dot(e["lm"].astype(BF16), w_bd)
            elif level < 5:
                y = _dot(jnp.concatenate([e["p"], e["lm"]], axis=0).astype(BF16), w_bd)
                e["p"] = e["p"] + y[:CH]
                e["lm"] = y[CH:]
            else:
                e["p"] = e["p"] + _dot(e["p"].astype(BF16), w_bd)

    for g, (d, b) in enumerate(groups):
        e = st[g]
        k = in_refs[d][1][b]
        v = in_refs[d][2][b]
        tinv = e["p"].astype(BF16)
        vb = (v * e["beta512"]).astype(BF16)
        kbe = (k * e["beta512"] * jnp.exp(e["ccol512"])).astype(BF16)
        e["uw"] = []
        for pr in pairs:
            h0, h1 = 2 * pr, 2 * pr + 1
            rhs = lambda h: jnp.concatenate([vb[:, h * GDN_D:(h + 1) * GDN_D], kbe[:, h * GDN_D:(h + 1) * GDN_D]], axis=1)
            e["uw"].append(_dot(tinv[:, pr * GDN_D:(pr + 1) * GDN_D], _pair_diag(rhs(h0), rhs(h1))))

    for g, (d, b) in enumerate(groups):
        e = st[g]
        q = in_refs[d][0][b]
        qe = q * jnp.exp(e["ccol512"])
        e["ws"] = []
        for pr in pairs:
            lhs = []
            for h in (2 * pr, 2 * pr + 1):
                w = e["uw"][pr][:, (2 * (h % 2) + 1) * GDN_D:(2 * (h % 2) + 2) * GDN_D]
                lhs.append(jnp.concatenate([w, qe[:, h * GDN_D:(h + 1) * GDN_D]], axis=0))
            sidx = (b * 2 + d) * GDN_HEADS + 2 * pr
            sbd = _pair_diag(s_ref[sidx].astype(BF16), s_ref[sidx + 1].astype(BF16))
            e["ws"].append(_dot(jnp.concatenate(lhs, axis=1).astype(BF16), sbd))

    for g, (d, b) in enumerate(groups):
        e = st[g]
        k = in_refs[d][1][b]
        last = masks[d]["last"]
        for pr in pairs:
            vnew = []
            for h in (2 * pr, 2 * pr + 1):
                u = e["uw"][pr][:, 2 * (h % 2) * GDN_D:(2 * (h % 2) + 1) * GDN_D]
                vnew.append((u - e["ws"][pr][:CH, (h % 2) * GDN_D:(h % 2 + 1) * GDN_D]).astype(BF16))
            ov = _dot(e["intra"][:, pr * GDN_D:(pr + 1) * GDN_D], _pair_diag(vnew[0], vnew[1]))
            out_refs[d][b, :, 2 * pr * GDN_D:(2 * pr + 2) * GDN_D] = e["ws"][pr][CH:, :] + ov
            for h in (2 * pr, 2 * pr + 1):
                hl = slice(h * GDN_D, (h + 1) * GDN_D)
                glast = e["ccols"][h][last:last + 1, :]
                kd = k[:, hl] * jnp.exp(glast - e["ccol512"][:, hl])
                sidx = (b * 2 + d) * GDN_HEADS + h
                s_ref[sidx] = s_ref[sidx] * jnp.exp(glast) + _dot(kd.T.astype(BF16), vnew[h % 2])


def _gdn_scan(q, k, v, bg, gt, *, b, s):
    n = s // CH
    fwd = lambda i: (0, i, 0)
    bwd = lambda i: (0, n - 1 - i, 0)
    fwd_t = lambda i: (0, i, 0, 0)
    bwd_t = lambda i: (0, n - 1 - i, 0, 0)
    wide = lambda im: pl.BlockSpec((b, CH, GDN_W), im)
    specs = lambda im, imt: [wide(im), wide(im), wide(im), pl.BlockSpec((b, CH, N_BA), im),
                             pl.BlockSpec((b, 1, 2 * GDN_HEADS, CH), imt)]
    out = jax.ShapeDtypeStruct((b, s, GDN_W), F32)
    return pl.pallas_call(
        _gdn_scan_body,
        out_shape=(out, out),
        grid=(n,),
        in_specs=specs(fwd, fwd_t) + specs(bwd, bwd_t),
        out_specs=(wide(fwd), wide(bwd)),
        scratch_shapes=[pltpu.VMEM((b * 2 * GDN_HEADS, GDN_D, GDN_D), F32)],
        compiler_params=_grid_params(1),
        name="gdn_scan",
    )(q, k, v, bg, gt, q, k, v, bg, gt)


def _merge_body(x_ref, of_ref, ob_ref, sz_ref, sg_ref, ot_ref, gnw_ref, wgp_ref, wmp_ref, wo_ref, o_ref):
    o = of_ref[...] + ob_ref[...]
    gnw = gnw_ref[...]
    heads = []
    for h in range(GDN_HEADS):
        oh = o[:, h * GDN_D:(h + 1) * GDN_D]
        heads.append(_rms(oh, gnw))
    on = jnp.concatenate(heads, axis=1) * sz_ref[...]
    ya = _dot(on.astype(BF16), wgp_ref[...])
    ot = ot_ref[0, :, 0].reshape(MLA_HEADS * MLA_V, -1)
    yb = _dot(ot.T.astype(BF16), wmp_ref[...])
    d = ya.shape[1]
    y = sg_ref[:, :d] * ya + sg_ref[:, d:] * yb
    o_ref[...] = x_ref[...] + _dot(y.astype(BF16), wo_ref[...])


def _merge(x, of, ob, sz, sg, ot, gnw, wgp, wmp, wo, *, b, s, tm):
    t, d = x.shape
    nsteps = s // tm
    row = lambda n: pl.BlockSpec((tm, n), lambda bi, i: (bi * nsteps + i, 0))
    return pl.pallas_call(
        _merge_body,
        out_shape=jax.ShapeDtypeStruct((t, d), F32),
        grid=(b, nsteps),
        in_specs=[row(d), row(GDN_W), row(GDN_W), row(GDN_W), row(2 * d),
                  pl.BlockSpec((1, MLA_HEADS, 1, MLA_V, tm), lambda bi, i: (bi, 0, i, 0, 0)),
                  _const_spec((1, GDN_D)), _const_spec(wgp.shape), _const_spec(wmp.shape), _const_spec(wo.shape)],
        out_specs=row(d),
        compiler_params=_grid_params(2),
        name="merge",
    )(x, of, ob, sz, sg, ot, gnw, wgp, wmp, wo)


def _pack_w_in(w):
    d = w.shape[0]
    zeros = lambda n: jnp.zeros((d, n), w.dtype)
    o = 4 * GDN_W
    ba = w[:, o:o + N_BA]
    o += N_BA
    cq = w[:, o:o + MLA_Q_LORA]
    o += MLA_Q_LORA
    ckv = w[:, o:o + MLA_KV_LORA]
    o += MLA_KV_LORA
    kr = w[:, o:o + MLA_ROPE]
    o += MLA_ROPE
    gates = w[:, o:]
    half = MLA_ROPE // 2
    tail = zeros(HEAD_LANES - MLA_NOPE - MLA_ROPE)
    kr_main = jnp.concatenate([zeros(MLA_NOPE), kr, tail], axis=1)
    kr_swap = jnp.concatenate([zeros(MLA_NOPE), kr[:, half:], kr[:, :half], tail], axis=1)
    packed = jnp.concatenate([w[:, :4 * GDN_W], ba, zeros(128 - N_BA), cq, ckv, kr_main, kr_swap, gates], axis=1)
    return packed.astype(BF16), ba.T.astype(BF16)


def _pack_w_uq(w):
    r = w.shape[0]
    qk = MLA_NOPE + MLA_ROPE
    half = MLA_ROPE // 2
    tail = jnp.zeros((r, HEAD_LANES - qk), w.dtype)
    znope = jnp.zeros((r, MLA_NOPE), w.dtype)
    main, swap = [], []
    for h in range(MLA_HEADS):
        nope = w[:, h * qk:h * qk + MLA_NOPE]
        rope = w[:, h * qk + MLA_NOPE:(h + 1) * qk]
        main += [nope, rope, tail]
        swap += [znope, rope[:, half:], rope[:, :half], tail]
    return jnp.concatenate(main, axis=1).T.astype(BF16), jnp.concatenate(swap, axis=1).T.astype(BF16)


def _pack_w_ukv(w):
    r = w.shape[0]
    hw = MLA_NOPE + MLA_V
    ks, vs = [], []
    for h in range(MLA_HEADS):
        ks += [w[:, h * hw:h * hw + MLA_NOPE], jnp.zeros((r, HEAD_LANES - MLA_NOPE), w.dtype)]
        vs += [w[:, h * hw + MLA_NOPE:(h + 1) * hw], jnp.zeros((r, V_ROWS - MLA_V), w.dtype)]
    return jnp.concatenate(ks, axis=1).astype(BF16), jnp.concatenate(vs, axis=1).T.astype(BF16)


def _ones_rows():
    idx = jnp.arange(MLA_HEADS * V_ROWS) % V_ROWS
    return (idx == MLA_V).astype(F32)[:, None]


def _lane_pad(v, lo, width):
    return jnp.zeros((1, width), v.dtype).at[0, lo:lo + v.shape[0]].set(v)


def kernel(x, positions, norm_ffn1, ffn1_w_gate, ffn1_w_up, ffn1_w_down, norm_mix, w_in, gdn_conv, gdn_A_log,
           gdn_dt_bias, gdn_norm, gdn_proj, mla_q_norm, mla_w_uq, mla_kv_norm, mla_w_ukv, mla_proj, w_out,
           norm_ffn2, ffn2_w_gate, ffn2_w_up, ffn2_w_down, final_norm):
    b, s, d = x.shape
    t = b * s
    depth = w_in.shape[0]
    tm = 512 if s % 2048 == 0 else 256
    ffn_tm = 256
    tq = tk = tm

    cos, sin, cost, sint = _rope_tables(positions, tm)
    ones = _ones_rows()
    xf = x.reshape(t, d)
    row = lambda v: v.reshape(1, -1)
    for l in range(depth):
        xf = _ffn(xf, row(norm_ffn1[l]), ffn1_w_gate[l].astype(BF16), ffn1_w_up[l].astype(BF16),
                  ffn1_w_down[l].astype(BF16), row(final_norm), final_norm=False, tm=ffn_tm)

        w_packed, w_bat = _pack_w_in(w_in[l])
        alog = gdn_A_log[l].reshape(-1)
        dtb = gdn_dt_bias[l].reshape(-1)
        qkv, sz, bg, gt, cqn, ckvn, kr, sg = _inproj(
            xf, row(norm_mix[l]), w_packed, w_bat, _lane_pad(alog, 2 * GDN_HEADS, 128),
            _lane_pad(dtb, 2 * GDN_HEADS, 128), alog[:, None], dtb[:, None], row(mla_q_norm[l]),
            row(mla_kv_norm[l]), cos, sin, tm=tm)

        qn, kn, vv = _gdn_prep(qkv, gdn_conv[l], s=s, tm=tm)
        seq = lambda a: a.reshape(b, s, a.shape[-1])
        gt_chunks = gt.reshape(2 * GDN_HEADS, b, s // CH, CH).transpose(1, 2, 0, 3)
        of, ob = _gdn_scan(seq(qn), seq(kn), seq(vv), seq(bg), gt_chunks, b=b, s=s)
        of, ob = of.reshape(t, GDN_W), ob.reshape(t, GDN_W)

        wqm, wqp = _pack_w_uq(mla_w_uq[l])
        wk, wvt = _pack_w_ukv(mla_w_ukv[l])
        qt, kk, vt = _mla_prep(cqn, ckvn, kr, cost, sint, wqm, wqp, wk, wvt, ones, b=b, s=s, tk=tk)
        ot = _attention(qt, kk, vt, b=b, s=s, tq=tq, tk=tk)

        xf = _merge(xf, of, ob, sz, sg, ot, row(gdn_norm[l]), gdn_proj[l].astype(BF16), mla_proj[l].astype(BF16),
                    w_out[l].astype(BF16), b=b, s=s, tm=tm)

        xf = _ffn(xf, row(norm_ffn2[l]), ffn2_w_gate[l].astype(BF16), ffn2_w_up[l].astype(BF16),
                  ffn2_w_down[l].astype(BF16), row(final_norm), final_norm=(l == depth - 1), tm=ffn_tm)
    return xf.reshape(b, s, d)
```

```python
import functools

import jax
import jax.numpy as jnp
from jax import lax
from jax.experimental import pallas as pl
from jax.experimental.pallas import tpu as pltpu

F32 = jnp.float32
BF16 = jnp.bfloat16

EPS = 1e-6
RES_HALF = 0.5
GDN_HEADS = 4
GDN_D = 128
GDN_CONV = 5
GDN_CHUNK = 64
MLA_HEADS = 8
MLA_NOPE = 64
MLA_ROPE = 32
MLA_V = 64
MLA_Q_LORA = 384
MLA_KV_LORA = 256
ROPE_THETA = 10000.0
HEAD_LANES = 128
V_ROWS = 80
LOG2E = 1.4426950408889634
NEG_BIG = -1e30

VMEM_LIMIT_BYTES = 56 * 1024 * 1024

NT_DIMS = (((1,), (1,)), ((), ()))


def _grid_params(n, flags=None):
    return pltpu.CompilerParams(dimension_semantics=("arbitrary",) * n, vmem_limit_bytes=VMEM_LIMIT_BYTES,
                                flags=flags)


def _const_spec(shape):
    nd = len(shape)
    return pl.BlockSpec(shape, lambda *_: (0,) * nd, pipeline_mode=pl.Buffered(1))


def _rms(x, w):
    return x * lax.rsqrt(jnp.mean(x * x, axis=-1, keepdims=True) + EPS) * w


def _silu(x):
    return x * jax.nn.sigmoid(x)


def _softplus(x):
    return jnp.maximum(x, 0.0) + jnp.log1p(jnp.exp(-jnp.abs(x)))


def _dot(a, b):
    return jnp.dot(a, b, preferred_element_type=F32)


def _ffn_body(x_ref, nw_ref, wg_ref, wu_ref, wd_ref, fw_ref, o_ref, *, final_norm):
    x = x_ref[...]
    hb = _rms(x, nw_ref[...]).astype(BF16)
    g = _dot(hb, wg_ref[...])
    u = _dot(hb, wu_ref[...])
    a = (_silu(g) * u).astype(BF16)
    y = x + RES_HALF * _dot(a, wd_ref[...])
    if final_norm:
        y = _rms(y, fw_ref[...])
    o_ref[...] = y


def _ffn(x, nw, wg, wu, wd, fw, *, final_norm, tm):
    t, d = x.shape
    ff = wg.shape[1]
    row = pl.BlockSpec((tm, d), lambda i: (i, 0))
    return pl.pallas_call(
        functools.partial(_ffn_body, final_norm=final_norm),
        out_shape=jax.ShapeDtypeStruct((t, d), F32),
        grid=(t // tm,),
        in_specs=[row, _const_spec((1, d)), _const_spec((d, ff)), _const_spec((d, ff)),
                  _const_spec((ff, d)), _const_spec((1, d))],
        out_specs=row,
        compiler_params=_grid_params(1),
        name="ffn",
    )(x, nw, wg, wu, wd, fw)


def _rope_body(post_ref, fcol_ref, cos_ref, sin_ref, cost_ref, sint_ref):
    ang = fcol_ref[...] * post_ref[0].astype(F32)
    c = jnp.cos(ang)
    s = jnp.sin(ang)
    tm = ang.shape[1]
    pad = HEAD_LANES - MLA_NOPE - MLA_ROPE
    cost = jnp.concatenate([jnp.ones((MLA_NOPE, tm), F32), c, c, jnp.ones((pad, tm), F32)], axis=0)
    sint = jnp.concatenate([jnp.zeros((MLA_NOPE, tm), F32), -s, s, jnp.zeros((pad, tm), F32)], axis=0)
    cost_ref[0] = cost
    sint_ref[0] = sint
    cos_ref[...] = cost.T
    sin_ref[...] = sint.T


def _rope_tables(positions, tm):
    b, s = positions.shape
    t = b * s
    inv_freq = jnp.power(ROPE_THETA, -jnp.arange(0, MLA_ROPE, 2, dtype=F32) / MLA_ROPE)
    nsteps = s // tm
    return pl.pallas_call(
        _rope_body,
        out_shape=(jax.ShapeDtypeStruct((t, HEAD_LANES), F32), jax.ShapeDtypeStruct((t, HEAD_LANES), F32),
                   jax.ShapeDtypeStruct((b, HEAD_LANES, s), F32), jax.ShapeDtypeStruct((b, HEAD_LANES, s), F32)),
        grid=(b, nsteps),
        in_specs=[pl.BlockSpec((1, 1, tm), lambda bi, i: (bi, 0, i)), _const_spec((MLA_ROPE // 2, 1))],
        out_specs=(pl.BlockSpec((tm, HEAD_LANES), lambda bi, i: (bi * nsteps + i, 0)),
                   pl.BlockSpec((tm, HEAD_LANES), lambda bi, i: (bi * nsteps + i, 0)),
                   pl.BlockSpec((1, HEAD_LANES, tm), lambda bi, i: (bi, 0, i)),
                   pl.BlockSpec((1, HEAD_LANES, tm), lambda bi, i: (bi, 0, i))),
        compiler_params=_grid_params(2),
        name="rope_tables",
    )(positions.reshape(b, 1, s), inv_freq[:, None])


GDN_W = GDN_HEADS * GDN_D
SEG_QKV = (0, 3 * GDN_W)
SEG_Z = (SEG_QKV[1], SEG_QKV[1] + GDN_W)
SEG_BA = (SEG_Z[1], SEG_Z[1] + 128)
SEG_CQ = (SEG_BA[1], SEG_BA[1] + MLA_Q_LORA)
SEG_CKV = (SEG_CQ[1], SEG_CQ[1] + MLA_KV_LORA)
SEG_KRM = (SEG_CKV[1], SEG_CKV[1] + HEAD_LANES)
SEG_KRP = (SEG_KRM[1], SEG_KRM[1] + HEAD_LANES)
SEG_GATE = (SEG_KRP[1], SEG_KRP[1] + 2048)
N_BA = 4 * GDN_HEADS


def _inproj_body(x_ref, nw_ref, w_ref, wbat_ref, alog_ref, dtb_ref, alogt_ref, dtbt_ref, qnw_ref, kvnw_ref,
                 cos_ref, sin_ref, qkv_ref, sz_ref, bg_ref, gt_ref, cqn_ref, ckvn_ref, kr_ref, sg_ref):
    hb = _rms(x_ref[...], nw_ref[...]).astype(BF16)

    def seg(bounds):
        return _dot(hb, w_ref[:, bounds[0]:bounds[1]])

    qkv_ref[...] = seg(SEG_QKV)
    sz_ref[...] = _silu(seg(SEG_Z)).astype(BF16)
    ba = seg(SEG_BA)
    lane = lax.broadcasted_iota(jnp.int32, ba.shape, 1)
    decay = -jnp.exp(alog_ref[...]) * _softplus(ba + dtb_ref[...])
    bg_ref[...] = jnp.where(lane < 2 * GDN_HEADS, jax.nn.sigmoid(ba), decay)[:, :N_BA]
    bat = lax.dot_general(wbat_ref[...], hb, NT_DIMS, preferred_element_type=F32)
    at = bat[2 * GDN_HEADS:, :]
    gt_ref[...] = -jnp.exp(alogt_ref[...]) * _softplus(at + dtbt_ref[...])
    cqn_ref[...] = _rms(seg(SEG_CQ), qnw_ref[...]).astype(BF16)
    ckvn_ref[...] = _rms(seg(SEG_CKV), kvnw_ref[...]).astype(BF16)
    kr_ref[...] = seg(SEG_KRM) * cos_ref[...] + seg(SEG_KRP) * sin_ref[...]
    sg_ref[...] = jax.nn.sigmoid(seg(SEG_GATE)).astype(BF16)


def _inproj(x, nw, w, wbat, alog, dtb, alogt, dtbt, qnw, kvnw, cos, sin, *, tm):
    t, d = x.shape
    row = lambda n: pl.BlockSpec((tm, n), lambda i: (i, 0))
    out_shape = (
        jax.ShapeDtypeStruct((t, 3 * GDN_W), F32),
        jax.ShapeDtypeStruct((t, GDN_W), BF16),
        jax.ShapeDtypeStruct((t, N_BA), F32),
        jax.ShapeDtypeStruct((2 * GDN_HEADS, t), F32),
        jax.ShapeDtypeStruct((t, MLA_Q_LORA), BF16),
        jax.ShapeDtypeStruct((t, MLA_KV_LORA), BF16),
        jax.ShapeDtypeStruct((t, HEAD_LANES), F32),
        jax.ShapeDtypeStruct((t, 2048), BF16),
    )
    out_specs = (row(3 * GDN_W), row(GDN_W), row(N_BA), pl.BlockSpec((2 * GDN_HEADS, tm), lambda i: (0, i)),
                 row(MLA_Q_LORA), row(MLA_KV_LORA), row(HEAD_LANES), row(2048))
    return pl.pallas_call(
        _inproj_body,
        out_shape=out_shape,
        grid=(t // tm,),
        in_specs=[row(d), _const_spec((1, d)), _const_spec(w.shape), _const_spec(wbat.shape),
                  _const_spec((1, 128)), _const_spec((1, 128)),
                  _const_spec((2 * GDN_HEADS, 1)), _const_spec((2 * GDN_HEADS, 1)),
                  _const_spec((1, MLA_Q_LORA)), _const_spec((1, MLA_KV_LORA)),
                  row(HEAD_LANES), row(HEAD_LANES)],
        out_specs=out_specs,
        compiler_params=_grid_params(1),
        name="inproj",
    )(x, nw, w, wbat, alog, dtb, alogt, dtbt, qnw, kvnw, cos, sin)


def _mla_prep_body(cqn_ref, ckvn_ref, kr_ref, cost_ref, sint_ref, wqm_ref, wqp_ref, wk_ref, wvt_ref, ones_ref,
                   qt_ref, k_ref, vt_ref):
    cqn = cqn_ref[...]
    qm = lax.dot_general(wqm_ref[...], cqn, NT_DIMS, preferred_element_type=F32)
    qp = lax.dot_general(wqp_ref[...], cqn, NT_DIMS, preferred_element_type=F32)
    qscale = (MLA_NOPE + MLA_ROPE) ** -0.5 * LOG2E
    cost = cost_ref[0] * qscale
    sint = sint_ref[0] * qscale
    ckvn = ckvn_ref[...]
    km = _dot(ckvn, wk_ref[...])
    kr = kr_ref[...]
    for h in range(MLA_HEADS):
        grp = slice(h * HEAD_LANES, (h + 1) * HEAD_LANES)
        qt_ref[0, h, 0] = (qm[grp, :] * cost + qp[grp, :] * sint).astype(BF16)
        k_ref[:, grp] = (km[:, grp] + kr).astype(BF16)
    vt = lax.dot_general(wvt_ref[...], ckvn, NT_DIMS, preferred_element_type=F32)
    vt_ref[0, 0] = (vt + ones_ref[...]).astype(BF16)


def _mla_prep(cqn, ckvn, kr, cost, sint, wqm, wqp, wk, wvt, ones, *, b, s, tk):
    t = b * s
    nsteps = s // tk
    hl = MLA_HEADS * HEAD_LANES
    vr = MLA_HEADS * V_ROWS
    row = lambda n: pl.BlockSpec((tk, n), lambda bi, i: (bi * nsteps + i, 0))
    tr = pl.BlockSpec((1, HEAD_LANES, tk), lambda bi, i: (bi, 0, i))
    return pl.pallas_call(
        _mla_prep_body,
        out_shape=(jax.ShapeDtypeStruct((b, MLA_HEADS, nsteps, HEAD_LANES, tk), BF16),
                   jax.ShapeDtypeStruct((t, hl), BF16),
                   jax.ShapeDtypeStruct((b, nsteps, vr, tk), BF16)),
        grid=(b, nsteps),
        in_specs=[row(MLA_Q_LORA), row(MLA_KV_LORA), row(HEAD_LANES), tr, tr,
                  _const_spec(wqm.shape), _const_spec(wqp.shape), _const_spec(wk.shape), _const_spec(wvt.shape),
                  _const_spec((vr, 1))],
        out_specs=(pl.BlockSpec((1, MLA_HEADS, 1, HEAD_LANES, tk), lambda bi, i: (bi, 0, i, 0, 0)), row(hl),
                   pl.BlockSpec((1, 1, vr, tk), lambda bi, i: (bi, i, 0, 0))),
        compiler_params=_grid_params(2),
        name="mla_prep",
    )(cqn, ckvn, kr, cost, sint, wqm, wqp, wk, wvt, ones)


ATTN_GENS = 3
ATTN_UNROLL = 2
ATTN_LAG = 2
KEY_SLICE = 128
PV_DEPTH = 256


def _attn_body(qt_ref, k_ref, vt_ref, ot_ref, s_buf, p_buf, bm_buf, a_buf, m_ref, acc_ref, *, nblk):
    nq = qt_ref.shape[2]
    tk = p_buf.shape[1]
    nblocks = nq * nblk
    nslices = tk // KEY_SLICE
    pv_every = PV_DEPTH // KEY_SLICE
    m_ref[...] = jnp.full(m_ref.shape, NEG_BIG, F32)
    acc_ref[...] = jnp.zeros(acc_ref.shape, F32)

    def step(t, phase, do_scores=True, do_softmax=True, do_pv=True):
        g_s, g_m, g_p = phase, (phase + 1) % ATTN_GENS, (phase + 2) % ATTN_GENS
        n_m, n_p = t - ATTN_LAG, t - 2 * ATTN_LAG
        if do_scores:
            j_s = t % nblk
            start = j_s * tk if isinstance(j_s, int) else pl.multiple_of(j_s * tk, tk)
            qt = qt_ref[0, 0, t // nblk]
        if do_softmax:
            m_old = jnp.where(n_m % nblk == 0, NEG_BIG, m_ref[...])
            m_new = jnp.maximum(m_old, bm_buf[g_m])
            a_buf[g_m] = jnp.exp2(m_old - m_new)
            m_ref[...] = m_new
        if do_pv:
            acc = acc_ref[...] * a_buf[g_p]
            vt = vt_ref.at[0, n_p % nblk]
        bm = None
        for c in range(nslices):
            rows = slice(c * KEY_SLICE, (c + 1) * KEY_SLICE)
            if do_pv and c % pv_every == pv_every - 1:
                deep = slice((c + 1 - pv_every) * KEY_SLICE, (c + 1) * KEY_SLICE)
                acc = acc + _dot(vt[:, deep], p_buf[g_p, deep, :])
            if do_softmax:
                p_buf[g_m, rows, :] = jnp.exp2(s_buf[g_m, rows, :] - m_new).astype(BF16)
            if do_scores:
                s = _dot(k_ref[pl.ds(start + c * KEY_SLICE, KEY_SLICE), :], qt)
                s_buf[g_s, rows, :] = s
                smax = jnp.max(s, axis=0, keepdims=True)
                bm = smax if bm is None else jnp.maximum(bm, smax)
        if do_scores:
            bm_buf[g_s] = bm
        if do_pv:
            acc_ref[...] = acc
            return n_p // nblk, acc[:MLA_V, :] / acc[MLA_V:MLA_V + 1, :]
        return None

    def write(out):
        if out is not None:
            ot_ref[0, 0, out[0]] = out[1]

    def static_step(t):
        write(step(t, t % ATTN_GENS, do_scores=t < nblocks, do_softmax=ATTN_LAG <= t < nblocks + ATTN_LAG,
                   do_pv=t >= 2 * ATTN_LAG))

    fill = 2 * ATTN_LAG
    for t in range(fill):
        static_step(t)
    per_iter = ATTN_GENS * ATTN_UNROLL
    nloop = (nblocks - fill) // per_iter

    def body(u, carry):
        t0 = fill + per_iter * u
        outs = [step(t0 + i, (fill + i) % ATTN_GENS) for i in range(per_iter)]
        for out in outs:
            write(out)
        return carry

    lax.fori_loop(0, nloop, body, 0)
    for t in range(fill + per_iter * nloop, nblocks + fill):
        static_step(t)


def _attention(qt, k, vt, *, b, s, tq, tk):
    nq = s // tq
    nk = s // tk
    assert tk % PV_DEPTH == 0 and PV_DEPTH % KEY_SLICE == 0
    return pl.pallas_call(
        functools.partial(_attn_body, nblk=nk),
        out_shape=jax.ShapeDtypeStruct((b, MLA_HEADS, nq, MLA_V, tq), F32),
        grid=(b, MLA_HEADS),
        in_specs=[pl.BlockSpec((1, 1, nq, HEAD_LANES, tq), lambda bi, h: (bi, h, 0, 0, 0)),
                  pl.BlockSpec((s, HEAD_LANES), lambda bi, h: (bi, h)),
                  pl.BlockSpec((1, nk, V_ROWS, tk), lambda bi, h: (bi, 0, h, 0))],
        out_specs=pl.BlockSpec((1, 1, nq, MLA_V, tq), lambda bi, h: (bi, h, 0, 0, 0)),
        scratch_shapes=[pltpu.VMEM((ATTN_GENS, tk, tq), F32), pltpu.VMEM((ATTN_GENS, tk, tq), BF16),
                        pltpu.VMEM((ATTN_GENS, 1, tq), F32), pltpu.VMEM((ATTN_GENS, 1, tq), F32),
                        pltpu.VMEM((1, tq), F32), pltpu.VMEM((V_ROWS, tq), F32)],
        compiler_params=_grid_params(2),
        name="attention",
    )(qt, k, vt)


HALO = 8


def _gdn_prep_body(prev_ref, cur_ref, next_ref, cw_ref, q_ref, k_ref, v_ref, buf_ref, *, tiles_per_seq):
    i = pl.program_id(0)
    tm = cur_ref.shape[0]
    first = (i % tiles_per_seq) == 0
    last = (i % tiles_per_seq) == tiles_per_seq - 1
    buf_ref[0:HALO, :] = jnp.where(first, 0.0, prev_ref[...])
    buf_ref[HALO:HALO + tm, :] = cur_ref[...]
    buf_ref[HALO + tm:2 * HALO + tm, :] = jnp.where(last, 0.0, next_ref[...])
    pad = GDN_CONV // 2
    for grp in range(3 * GDN_HEADS):
        lanes = slice(grp * GDN_D, (grp + 1) * GDN_D)
        acc = None
        for j in range(GDN_CONV):
            lo = HALO - pad + j
            term = buf_ref[lo:lo + tm, lanes] * cw_ref[j:j + 1, lanes]
            acc = term if acc is None else acc + term
        y = _silu(acc)
        if grp < 2 * GDN_HEADS:
            y = y * lax.rsqrt(jnp.sum(y * y, axis=-1, keepdims=True) + EPS)
        if grp < GDN_HEADS:
            q_ref[:, lanes] = y * GDN_D ** -0.5
        elif grp < 2 * GDN_HEADS:
            k_ref[:, slice(lanes.start - GDN_W, lanes.stop - GDN_W)] = y
        else:
            v_ref[:, slice(lanes.start - 2 * GDN_W, lanes.stop - 2 * GDN_W)] = y


def _gdn_prep(qkv, conv_w, *, s, tm):
    t, c = qkv.shape
    tiles_per_seq = s // tm
    hb = tm // HALO
    nh = t // HALO
    out = jax.ShapeDtypeStruct((t, GDN_W), F32)
    row = pl.BlockSpec((tm, GDN_W), lambda i: (i, 0))
    return pl.pallas_call(
        functools.partial(_gdn_prep_body, tiles_per_seq=tiles_per_seq),
        out_shape=(out, out, out),
        grid=(t // tm,),
        in_specs=[pl.BlockSpec((HALO, c), lambda i: (jnp.maximum(i * hb - 1, 0), 0)),
                  pl.BlockSpec((tm, c), lambda i: (i, 0)),
                  pl.BlockSpec((HALO, c), lambda i: (jnp.minimum((i + 1) * hb, nh - 1), 0)),
                  _const_spec(conv_w.shape)],
        out_specs=(row, row, row),
        scratch_shapes=[pltpu.VMEM((tm + 2 * HALO, c), F32)],
        compiler_params=_grid_params(1),
        name="gdn_prep",
    )(qkv, qkv, qkv, conv_w)


CH = GDN_CHUNK
HC = GDN_HEADS * CH


def _split3(x):
    hi = x.astype(BF16)
    r = x - hi.astype(F32)
    mid = r.astype(BF16)
    lo = (r - mid.astype(F32)).astype(BF16)
    return hi, mid, lo


def _bcast_cols(cols, width):
    if width == 128:
        return jnp.concatenate([jnp.broadcast_to(c, (CH, 128)) for c in cols], axis=1)
    lane = lax.broadcasted_iota(jnp.int32, (CH, 128), 1)
    lo = lane < 64
    pair = lambda a, b: jnp.where(lo, jnp.broadcast_to(a, (CH, 128)), jnp.broadcast_to(b, (CH, 128)))
    return jnp.concatenate([pair(cols[0], cols[1]), pair(cols[2], cols[3])], axis=1)


def _dir_masks(d):
    r64 = lax.broadcasted_iota(jnp.int32, (CH, CH), 0)
    c64 = lax.broadcasted_iota(jnp.int32, (CH, CH), 1)
    r256 = lax.broadcasted_iota(jnp.int32, (CH, HC), 0)
    c256 = lax.broadcasted_iota(jnp.int32, (CH, HC), 1) & (CH - 1)
    if d == 0:
        tri, incl, strict, trit, last = (c64 <= r64), (c256 <= r256), (c256 < r256), (r256 <= c256), CH - 1
    else:
        tri, incl, strict, trit, last = (c64 >= r64), (c256 >= r256), (c256 > r256), (r256 >= c256), 0
    return dict(tri=tri.astype(BF16), trit=trit.astype(BF16), incl=incl, strict=strict, last=last)


def _pair_diag(a, b):
    z = jnp.zeros(a.shape, a.dtype)
    return jnp.concatenate([jnp.concatenate([a, z], axis=1), jnp.concatenate([z, b], axis=1)], axis=0)


def _gdn_scan_body(qf_ref, kf_ref, vf_ref, bgf_ref, gtf_ref, qb_ref, kb_ref, vb_ref, bgb_ref, gtb_ref,
                   of_ref, ob_ref, s_ref):
    @pl.when(pl.program_id(0) == 0)
    def _():
        s_ref[...] = jnp.zeros(s_ref.shape, F32)

    nb = qf_ref.shape[0]
    in_refs = ((qf_ref, kf_ref, vf_ref, bgf_ref, gtf_ref), (qb_ref, kb_ref, vb_ref, bgb_ref, gtb_ref))
    out_refs = (of_ref, ob_ref)
    masks = (_dir_masks(0), _dir_masks(1))
    groups = [(d, b) for b in range(nb) for d in (0, 1)]
    heads = range(GDN_HEADS)
    pairs = range(GDN_HEADS // 2)

    r256 = lax.broadcasted_iota(jnp.int32, (CH, HC), 0)
    c256 = lax.broadcasted_iota(jnp.int32, (CH, HC), 1) & (CH - 1)
    eye = (c256 == r256).astype(F32)
    bd_mask = ((lax.broadcasted_iota(jnp.int32, (HC, HC), 0) >> 6)
               == (lax.broadcasted_iota(jnp.int32, (HC, HC), 1) >> 6)).astype(BF16)
    kbd_mask = ((lax.broadcasted_iota(jnp.int32, (HC, GDN_W), 0) >> 6)
                == (lax.broadcasted_iota(jnp.int32, (HC, GDN_W), 1) >> 7)).astype(BF16)
    r8 = lax.broadcasted_iota(jnp.int32, (2 * GDN_HEADS, HC), 0)
    h8 = lax.broadcasted_iota(jnp.int32, (2 * GDN_HEADS, HC), 1) >> 6

    st = []
    for d, b in groups:
        mk = masks[d]
        bg = in_refs[d][3][b]
        gt = in_refs[d][4][b, 0]
        bh, bm, bl = _split3(bg)
        cs = _dot(mk["tri"], bh) + _dot(mk["tri"], bm) + _dot(mk["tri"], bl)
        gh, gm, gl = _split3(gt)
        cst = _dot(gh, mk["trit"]) + _dot(gm, mk["trit"]) + _dot(gl, mk["trit"])
        st.append(dict(bg=bg, cs=cs, cst=cst))

    for g, (d, b) in enumerate(groups):
        mk, e = masks[d], st[g]
        crow = jnp.sum(jnp.where(r8 == d * GDN_HEADS + h8, e["cst"], 0.0), axis=0, keepdims=True)
        goff = 2 * GDN_HEADS + d * GDN_HEADS
        ccols = [e["cs"][:, goff + h:goff + h + 1] for h in heads]
        betas = [e["bg"][:, d * GDN_HEADS + h:d * GDN_HEADS + h + 1] for h in heads]
        e["ccols"] = ccols
        e["ccol512"] = _bcast_cols(ccols, GDN_D)
        e["beta512"] = _bcast_cols(betas, GDN_D)
        e["decay"] = jnp.exp(jnp.where(mk["incl"], _bcast_cols(ccols, CH) - crow, -jnp.inf))

    for g, (d, b) in enumerate(groups):
        mk, e = masks[d], st[g]
        q = in_refs[d][0][b]
        k = in_refs[d][1][b]
        kb = k * e["beta512"]
        kbd = jnp.concatenate([k.astype(BF16)] * GDN_HEADS, axis=0) * kbd_mask
        kq = lax.dot_general(jnp.concatenate([kb, q], axis=0).astype(BF16), kbd, NT_DIMS,
                             preferred_element_type=F32)
        neg_l = jnp.where(mk["strict"], -(kq[:CH] * e["decay"]), 0.0)
        e["intra"] = (kq[CH:] * e["decay"]).astype(BF16)
        e["p"] = eye + neg_l
        e["lm"] = neg_l

    for level in range(6):
        for e in st:
            w_bd = jnp.concatenate([e["lm"].astype(BF16)] * GDN_HEADS, axis=0) * bd_mask
            if level == 0:
                e["lm"] = _dot(e["lm"].astype(BF16), w_bd)
            elif level < 5:
                y = _dot(jnp.concatenate([e["p"], e["lm"]], axis=0).astype(BF16), w_bd)
                e["p"] = e["p"] + y[:CH]
                e["lm"] = y[CH:]
            else:
                e["p"] = e["p"] + _dot(e["p"].astype(BF16), w_bd)

    for g, (d, b) in enumerate(groups):
        e = st[g]
        k = in_refs[d][1][b]
        v = in_refs[d][2][b]
        tinv = e["p"].astype(BF16)
        vb = (v * e["beta512"]).astype(BF16)
        kbe = (k * e["beta512"] * jnp.exp(e["ccol512"])).astype(BF16)
        e["uw"] = []
        for pr in pairs:
            h0, h1 = 2 * pr, 2 * pr + 1
            rhs = lambda h: jnp.concatenate([vb[:, h * GDN_D:(h + 1) * GDN_D], kbe[:, h * GDN_D:(h + 1) * GDN_D]], axis=1)
            e["uw"].append(_dot(tinv[:, pr * GDN_D:(pr + 1) * GDN_D], _pair_diag(rhs(h0), rhs(h1))))

    for g, (d, b) in enumerate(groups):
        e = st[g]
        q = in_refs[d][0][b]
        qe = q * jnp.exp(e["ccol512"])
        e["ws"] = []
        for pr in pairs:
            lhs = []
            for h in (2 * pr, 2 * pr + 1):
                w = e["uw"][pr][:, (2 * (h % 2) + 1) * GDN_D:(2 * (h % 2) + 2) * GDN_D]
                lhs.append(jnp.concatenate([w, qe[:, h * GDN_D:(h + 1) * GDN_D]], axis=0))
            sidx = (b * 2 + d) * GDN_HEADS + 2 * pr
            sbd = _pair_diag(s_ref[sidx].astype(BF16), s_ref[sidx + 1].astype(BF16))
            e["ws"].append(_dot(jnp.concatenate(lhs, axis=1).astype(BF16), sbd))

    for g, (d, b) in enumerate(groups):
        e = st[g]
        k = in_refs[d][1][b]
        last = masks[d]["last"]
        for pr in pairs:
            vnew = []
            for h in (2 * pr, 2 * pr + 1):
                u = e["uw"][pr][:, 2 * (h % 2) * GDN_D:(2 * (h % 2) + 1) * GDN_D]
                vnew.append((u - e["ws"][pr][:CH, (h % 2) * GDN_D:(h % 2 + 1) * GDN_D]).astype(BF16))
            ov = _dot(e["intra"][:, pr * GDN_D:(pr + 1) * GDN_D], _pair_diag(vnew[0], vnew[1]))
            out_refs[d][b, :, 2 * pr * GDN_D:(2 * pr + 2) * GDN_D] = e["ws"][pr][CH:, :] + ov
            for h in (2 * pr, 2 * pr + 1):
                hl = slice(h * GDN_D, (h + 1) * GDN_D)
                glast = e["ccols"][h][last:last + 1, :]
                kd = k[:, hl] * jnp.exp(glast - e["ccol512"][:, hl])
                sidx = (b * 2 + d) * GDN_HEADS + h
                s_ref[sidx] = s_ref[sidx] * jnp.exp(glast) + _dot(kd.T.astype(BF16), vnew[h % 2])


def _gdn_scan(q, k, v, bg, gt, *, b, s):
    n = s // CH
    fwd = lambda i: (0, i, 0)
    bwd = lambda i: (0, n - 1 - i, 0)
    fwd_t = lambda i: (0, i, 0, 0)
    bwd_t = lambda i: (0, n - 1 - i, 0, 0)
    wide = lambda im: pl.BlockSpec((b, CH, GDN_W), im)
    specs = lambda im, imt: [wide(im), wide(im), wide(im), pl.BlockSpec((b, CH, N_BA), im),
                             pl.BlockSpec((b, 1, 2 * GDN_HEADS, CH), imt)]
    out = jax.ShapeDtypeStruct((b, s, GDN_W), F32)
    return pl.pallas_call(
        _gdn_scan_body,
        out_shape=(out, out),
        grid=(n,),
        in_specs=specs(fwd, fwd_t) + specs(bwd, bwd_t),
        out_specs=(wide(fwd), wide(bwd)),
        scratch_shapes=[pltpu.VMEM((b * 2 * GDN_HEADS, GDN_D, GDN_D), F32)],
        compiler_params=_grid_params(1),
        name="gdn_scan",
    )(q, k, v, bg, gt, q, k, v, bg, gt)


def _merge_body(x_ref, of_ref, ob_ref, sz_ref, sg_ref, ot_ref, gnw_ref, wgp_ref, wmp_ref, wo_ref, o_ref):
    o = of_ref[...] + ob_ref[...]
    gnw = gnw_ref[...]
    heads = []
    for h in range(GDN_HEADS):
        oh = o[:, h * GDN_D:(h + 1) * GDN_D]
        heads.append(_rms(oh, gnw))
    on = jnp.concatenate(heads, axis=1) * sz_ref[...]
    ya = _dot(on.astype(BF16), wgp_ref[...])
    ot = ot_ref[0, :, 0].reshape(MLA_HEADS * MLA_V, -1)
    yb = _dot(ot.T.astype(BF16), wmp_ref[...])
    d = ya.shape[1]
    y = sg_ref[:, :d] * ya + sg_ref[:, d:] * yb
    o_ref[...] = x_ref[...] + _dot(y.astype(BF16), wo_ref[...])


def _merge(x, of, ob, sz, sg, ot, gnw, wgp, wmp, wo, *, b, s, tm):
    t, d = x.shape
    nsteps = s // tm
    row = lambda n: pl.BlockSpec((tm, n), lambda bi, i: (bi * nsteps + i, 0))
    return pl.pallas_call(
        _merge_body,
        out_shape=jax.ShapeDtypeStruct((t, d), F32),
        grid=(b, nsteps),
        in_specs=[row(d), row(GDN_W), row(GDN_W), row(GDN_W), row(2 * d),
                  pl.BlockSpec((1, MLA_HEADS, 1, MLA_V, tm), lambda bi, i: (bi, 0, i, 0, 0)),
                  _const_spec((1, GDN_D)), _const_spec(wgp.shape), _const_spec(wmp.shape), _const_spec(wo.shape)],
        out_specs=row(d),
        compiler_params=_grid_params(2),
        name="merge",
    )(x, of, ob, sz, sg, ot, gnw, wgp, wmp, wo)


def _pack_w_in(w):
    d = w.shape[0]
    zeros = lambda n: jnp.zeros((d, n), w.dtype)
    o = 4 * GDN_W
    ba = w[:, o:o + N_BA]
    o += N_BA
    cq = w[:, o:o + MLA_Q_LORA]
    o += MLA_Q_LORA
    ckv = w[:, o:o + MLA_KV_LORA]
    o += MLA_KV_LORA
    kr = w[:, o:o + MLA_ROPE]
    o += MLA_ROPE
    gates = w[:, o:]
    half = MLA_ROPE // 2
    tail = zeros(HEAD_LANES - MLA_NOPE - MLA_ROPE)
    kr_main = jnp.concatenate([zeros(MLA_NOPE), kr, tail], axis=1)
    kr_swap = jnp.concatenate([zeros(MLA_NOPE), kr[:, half:], kr[:, :half], tail], axis=1)
    packed = jnp.concatenate([w[:, :4 * GDN_W], ba, zeros(128 - N_BA), cq, ckv, kr_main, kr_swap, gates], axis=1)
    return packed.astype(BF16), ba.T.astype(BF16)


def _pack_w_uq(w):
    r = w.shape[0]
    qk = MLA_NOPE + MLA_ROPE
    half = MLA_ROPE // 2
    tail = jnp.zeros((r, HEAD_LANES - qk), w.dtype)
    znope = jnp.zeros((r, MLA_NOPE), w.dtype)
    main, swap = [], []
    for h in range(MLA_HEADS):
        nope = w[:, h * qk:h * qk + MLA_NOPE]
        rope = w[:, h * qk + MLA_NOPE:(h + 1) * qk]
        main += [nope, rope, tail]
        swap += [znope, rope[:, half:], rope[:, :half], tail]
    return jnp.concatenate(main, axis=1).T.astype(BF16), jnp.concatenate(swap, axis=1).T.astype(BF16)


def _pack_w_ukv(w):
    r = w.shape[0]
    hw = MLA_NOPE + MLA_V
    ks, vs = [], []
    for h in range(MLA_HEADS):
        ks += [w[:, h * hw:h * hw + MLA_NOPE], jnp.zeros((r, HEAD_LANES - MLA_NOPE), w.dtype)]
        vs += [w[:, h * hw + MLA_NOPE:(h + 1) * hw], jnp.zeros((r, V_ROWS - MLA_V), w.dtype)]
    return jnp.concatenate(ks, axis=1).astype(BF16), jnp.concatenate(vs, axis=1).T.astype(BF16)


def _ones_rows():
    idx = jnp.arange(MLA_HEADS * V_ROWS) % V_ROWS
    return (idx == MLA_V).astype(F32)[:, None]


def _lane_pad(v, lo, width):
    return jnp.zeros((1, width), v.dtype).at[0, lo:lo + v.shape[0]].set(v)


def kernel(x, positions, norm_ffn1, ffn1_w_gate, ffn1_w_up, ffn1_w_down, norm_mix, w_in, gdn_conv, gdn_A_log,
           gdn_dt_bias, gdn_norm, gdn_proj, mla_q_norm, mla_w_uq, mla_kv_norm, mla_w_ukv, mla_proj, w_out,
           norm_ffn2, ffn2_w_gate, ffn2_w_up, ffn2_w_down, final_norm):
    b, s, d = x.shape
    t = b * s
    depth = w_in.shape[0]
    tm = 512 if s % 2048 == 0 else 256
    ffn_tm = 512
    tq = tk = tm

    cos, sin, cost, sint = _rope_tables(positions, tm)
    ones = _ones_rows()
    xf = x.reshape(t, d)
    row = lambda v: v.reshape(1, -1)
    for l in range(depth):
        xf = _ffn(xf, row(norm_ffn1[l]), ffn1_w_gate[l].astype(BF16), ffn1_w_up[l].astype(BF16),
                  ffn1_w_down[l].astype(BF16), row(final_norm), final_norm=False, tm=ffn_tm)

        w_packed, w_bat = _pack_w_in(w_in[l])
        alog = gdn_A_log[l].reshape(-1)
        dtb = gdn_dt_bias[l].reshape(-1)
        qkv, sz, bg, gt, cqn, ckvn, kr, sg = _inproj(
            xf, row(norm_mix[l]), w_packed, w_bat, _lane_pad(alog, 2 * GDN_HEADS, 128),
            _lane_pad(dtb, 2 * GDN_HEADS, 128), alog[:, None], dtb[:, None], row(mla_q_norm[l]),
            row(mla_kv_norm[l]), cos, sin, tm=tm)

        qn, kn, vv = _gdn_prep(qkv, gdn_conv[l], s=s, tm=tm)
        seq = lambda a: a.reshape(b, s, a.shape[-1])
        gt_chunks = gt.reshape(2 * GDN_HEADS, b, s // CH, CH).transpose(1, 2, 0, 3)
        of, ob = _gdn_scan(seq(qn), seq(kn), seq(vv), seq(bg), gt_chunks, b=b, s=s)
        of, ob = of.reshape(t, GDN_W), ob.reshape(t, GDN_W)

        wqm, wqp = _pack_w_uq(mla_w_uq[l])
        wk, wvt = _pack_w_ukv(mla_w_ukv[l])
        qt, kk, vt = _mla_prep(cqn, ckvn, kr, cost, sint, wqm, wqp, wk, wvt, ones, b=b, s=s, tk=tk)
        ot = _attention(qt, kk, vt, b=b, s=s, tq=tq, tk=tk)

        xf = _merge(xf, of, ob, sz, sg, ot, row(gdn_norm[l]), gdn_proj[l].astype(BF16), mla_proj[l].astype(BF16),
                    w_out[l].astype(BF16), b=b, s=s, tm=tm)

        xf = _ffn(xf, row(norm_ffn2[l]), ffn2_w_gate[l].astype(BF16), ffn2_w_up[l].astype(BF16),
                  ffn2_w_down[l].astype(BF16), row(final_norm), final_norm=(l == depth - 1), tm=ffn_tm)
    return xf.reshape(b, s, d)
```

```python
import functools

import jax
import jax.numpy as jnp
from jax import lax
from jax.experimental import pallas as pl
from jax.experimental.pallas import tpu as pltpu

F32 = jnp.float32
BF16 = jnp.bfloat16

EPS = 1e-6
RES_HALF = 0.5
GDN_HEADS = 4
GDN_D = 128
GDN_CONV = 5
GDN_CHUNK = 64
MLA_HEADS = 8
MLA_NOPE = 64
MLA_ROPE = 32
MLA_V = 64
MLA_Q_LORA = 384
MLA_KV_LORA = 256
ROPE_THETA = 10000.0
HEAD_LANES = 128
V_ROWS = 80
LOG2E = 1.4426950408889634
NEG_BIG = -1e30

VMEM_LIMIT_BYTES = 56 * 1024 * 1024

NT_DIMS = (((1,), (1,)), ((), ()))


def _grid_params(n, flags=None):
    return pltpu.CompilerParams(dimension_semantics=("arbitrary",) * n, vmem_limit_bytes=VMEM_LIMIT_BYTES,
                                flags=flags)


def _const_spec(shape):
    nd = len(shape)
    return pl.BlockSpec(shape, lambda *_: (0,) * nd, pipeline_mode=pl.Buffered(1))


def _rms(x, w):
    return x * lax.rsqrt(jnp.mean(x * x, axis=-1, keepdims=True) + EPS) * w


def _silu(x):
    return x * jax.nn.sigmoid(x)


def _softplus(x):
    return jnp.maximum(x, 0.0) + jnp.log1p(jnp.exp(-jnp.abs(x)))


def _dot(a, b):
    return jnp.dot(a, b, preferred_element_type=F32)


def _ffn_body(x_ref, nw_ref, wg_ref, wu_ref, wd_ref, fw_ref, o_ref, *, final_norm):
    x = x_ref[...]
    hb = _rms(x, nw_ref[...]).astype(BF16)
    g = _dot(hb, wg_ref[...])
    u = _dot(hb, wu_ref[...])
    a = (_silu(g) * u).astype(BF16)
    y = x + RES_HALF * _dot(a, wd_ref[...])
    if final_norm:
        y = _rms(y, fw_ref[...])
    o_ref[...] = y


def _ffn(x, nw, wg, wu, wd, fw, *, final_norm, tm):
    t, d = x.shape
    ff = wg.shape[1]
    row = pl.BlockSpec((tm, d), lambda i: (i, 0))
    return pl.pallas_call(
        functools.partial(_ffn_body, final_norm=final_norm),
        out_shape=jax.ShapeDtypeStruct((t, d), F32),
        grid=(t // tm,),
        in_specs=[row, _const_spec((1, d)), _const_spec((d, ff)), _const_spec((d, ff)),
                  _const_spec((ff, d)), _const_spec((1, d))],
        out_specs=row,
        compiler_params=_grid_params(1),
        name="ffn",
    )(x, nw, wg, wu, wd, fw)


def _rope_body(post_ref, fcol_ref, cos_ref, sin_ref, cost_ref, sint_ref):
    ang = fcol_ref[...] * post_ref[0].astype(F32)
    c = jnp.cos(ang)
    s = jnp.sin(ang)
    tm = ang.shape[1]
    pad = HEAD_LANES - MLA_NOPE - MLA_ROPE
    cost = jnp.concatenate([jnp.ones((MLA_NOPE, tm), F32), c, c, jnp.ones((pad, tm), F32)], axis=0)
    sint = jnp.concatenate([jnp.zeros((MLA_NOPE, tm), F32), -s, s, jnp.zeros((pad, tm), F32)], axis=0)
    cost_ref[0] = cost
    sint_ref[0] = sint
    cos_ref[...] = cost.T
    sin_ref[...] = sint.T


def _rope_tables(positions, tm):
    b, s = positions.shape
    t = b * s
    inv_freq = jnp.power(ROPE_THETA, -jnp.arange(0, MLA_ROPE, 2, dtype=F32) / MLA_ROPE)
    nsteps = s // tm
    return pl.pallas_call(
        _rope_body,
        out_shape=(jax.ShapeDtypeStruct((t, HEAD_LANES), F32), jax.ShapeDtypeStruct((t, HEAD_LANES), F32),
                   jax.ShapeDtypeStruct((b, HEAD_LANES, s), F32), jax.ShapeDtypeStruct((b, HEAD_LANES, s), F32)),
        grid=(b, nsteps),
        in_specs=[pl.BlockSpec((1, 1, tm), lambda bi, i: (bi, 0, i)), _const_spec((MLA_ROPE // 2, 1))],
        out_specs=(pl.BlockSpec((tm, HEAD_LANES), lambda bi, i: (bi * nsteps + i, 0)),
                   pl.BlockSpec((tm, HEAD_LANES), lambda bi, i: (bi * nsteps + i, 0)),
                   pl.BlockSpec((1, HEAD_LANES, tm), lambda bi, i: (bi, 0, i)),
                   pl.BlockSpec((1, HEAD_LANES, tm), lambda bi, i: (bi, 0, i))),
        compiler_params=_grid_params(2),
        name="rope_tables",
    )(positions.reshape(b, 1, s), inv_freq[:, None])


GDN_W = GDN_HEADS * GDN_D
SEG_QKV = (0, 3 * GDN_W)
SEG_Z = (SEG_QKV[1], SEG_QKV[1] + GDN_W)
SEG_BA = (SEG_Z[1], SEG_Z[1] + 128)
SEG_CQ = (SEG_BA[1], SEG_BA[1] + MLA_Q_LORA)
SEG_CKV = (SEG_CQ[1], SEG_CQ[1] + MLA_KV_LORA)
SEG_KRM = (SEG_CKV[1], SEG_CKV[1] + HEAD_LANES)
SEG_KRP = (SEG_KRM[1], SEG_KRM[1] + HEAD_LANES)
SEG_GATE = (SEG_KRP[1], SEG_KRP[1] + 2048)
N_BA = 4 * GDN_HEADS


def _inproj_body(x_ref, nw_ref, w_ref, wbat_ref, alog_ref, dtb_ref, alogt_ref, dtbt_ref, qnw_ref, kvnw_ref,
                 cos_ref, sin_ref, qkv_ref, sz_ref, bg_ref, gt_ref, cqn_ref, ckvn_ref, kr_ref, sg_ref):
    hb = _rms(x_ref[...], nw_ref[...]).astype(BF16)

    def seg(bounds):
        return _dot(hb, w_ref[:, bounds[0]:bounds[1]])

    qkv_ref[...] = seg(SEG_QKV)
    sz_ref[...] = _silu(seg(SEG_Z)).astype(BF16)
    ba = seg(SEG_BA)
    lane = lax.broadcasted_iota(jnp.int32, ba.shape, 1)
    decay = -jnp.exp(alog_ref[...]) * _softplus(ba + dtb_ref[...])
    bg_ref[...] = jnp.where(lane < 2 * GDN_HEADS, jax.nn.sigmoid(ba), decay)[:, :N_BA]
    bat = lax.dot_general(wbat_ref[...], hb, NT_DIMS, preferred_element_type=F32)
    at = bat[2 * GDN_HEADS:, :]
    gt_ref[...] = -jnp.exp(alogt_ref[...]) * _softplus(at + dtbt_ref[...])
    cqn_ref[...] = _rms(seg(SEG_CQ), qnw_ref[...]).astype(BF16)
    ckvn_ref[...] = _rms(seg(SEG_CKV), kvnw_ref[...]).astype(BF16)
    kr_ref[...] = seg(SEG_KRM) * cos_ref[...] + seg(SEG_KRP) * sin_ref[...]
    sg_ref[...] = jax.nn.sigmoid(seg(SEG_GATE)).astype(BF16)


def _inproj(x, nw, w, wbat, alog, dtb, alogt, dtbt, qnw, kvnw, cos, sin, *, tm):
    t, d = x.shape
    row = lambda n: pl.BlockSpec((tm, n), lambda i: (i, 0))
    out_shape = (
        jax.ShapeDtypeStruct((t, 3 * GDN_W), F32),
        jax.ShapeDtypeStruct((t, GDN_W), BF16),
        jax.ShapeDtypeStruct((t, N_BA), F32),
        jax.ShapeDtypeStruct((2 * GDN_HEADS, t), F32),
        jax.ShapeDtypeStruct((t, MLA_Q_LORA), BF16),
        jax.ShapeDtypeStruct((t, MLA_KV_LORA), BF16),
        jax.ShapeDtypeStruct((t, HEAD_LANES), F32),
        jax.ShapeDtypeStruct((t, 2048), BF16),
    )
    out_specs = (row(3 * GDN_W), row(GDN_W), row(N_BA), pl.BlockSpec((2 * GDN_HEADS, tm), lambda i: (0, i)),
                 row(MLA_Q_LORA), row(MLA_KV_LORA), row(HEAD_LANES), row(2048))
    return pl.pallas_call(
        _inproj_body,
        out_shape=out_shape,
        grid=(t // tm,),
        in_specs=[row(d), _const_spec((1, d)), _const_spec(w.shape), _const_spec(wbat.shape),
                  _const_spec((1, 128)), _const_spec((1, 128)),
                  _const_spec((2 * GDN_HEADS, 1)), _const_spec((2 * GDN_HEADS, 1)),
                  _const_spec((1, MLA_Q_LORA)), _const_spec((1, MLA_KV_LORA)),
                  row(HEAD_LANES), row(HEAD_LANES)],
        out_specs=out_specs,
        compiler_params=_grid_params(1),
        name="inproj",
    )(x, nw, w, wbat, alog, dtb, alogt, dtbt, qnw, kvnw, cos, sin)


def _mla_prep_body(cqn_ref, ckvn_ref, kr_ref, cost_ref, sint_ref, wqm_ref, wqp_ref, wk_ref, wvt_ref, ones_ref,
                   qt_ref, k_ref, vt_ref):
    cqn = cqn_ref[...]
    qm = lax.dot_general(wqm_ref[...], cqn, NT_DIMS, preferred_element_type=F32)
    qp = lax.dot_general(wqp_ref[...], cqn, NT_DIMS, preferred_element_type=F32)
    qscale = (MLA_NOPE + MLA_ROPE) ** -0.5 * LOG2E
    cost = cost_ref[0] * qscale
    sint = sint_ref[0] * qscale
    ckvn = ckvn_ref[...]
    km = _dot(ckvn, wk_ref[...])
    kr = kr_ref[...]
    for h in range(MLA_HEADS):
        grp = slice(h * HEAD_LANES, (h + 1) * HEAD_LANES)
        qt_ref[0, h, 0] = (qm[grp, :] * cost + qp[grp, :] * sint).astype(BF16)
        k_ref[:, grp] = (km[:, grp] + kr).astype(BF16)
    vt = lax.dot_general(wvt_ref[...], ckvn, NT_DIMS, preferred_element_type=F32)
    vt_ref[0, 0] = (vt + ones_ref[...]).astype(BF16)


def _mla_prep(cqn, ckvn, kr, cost, sint, wqm, wqp, wk, wvt, ones, *, b, s, tk):
    t = b * s
    nsteps = s // tk
    hl = MLA_HEADS * HEAD_LANES
    vr = MLA_HEADS * V_ROWS
    row = lambda n: pl.BlockSpec((tk, n), lambda bi, i: (bi * nsteps + i, 0))
    tr = pl.BlockSpec((1, HEAD_LANES, tk), lambda bi, i: (bi, 0, i))
    return pl.pallas_call(
        _mla_prep_body,
        out_shape=(jax.ShapeDtypeStruct((b, MLA_HEADS, nsteps, HEAD_LANES, tk), BF16),
                   jax.ShapeDtypeStruct((t, hl), BF16),
                   jax.ShapeDtypeStruct((b, nsteps, vr, tk), BF16)),
        grid=(b, nsteps),
        in_specs=[row(MLA_Q_LORA), row(MLA_KV_LORA), row(HEAD_LANES), tr, tr,
                  _const_spec(wqm.shape), _const_spec(wqp.shape), _const_spec(wk.shape), _const_spec(wvt.shape),
                  _const_spec((vr, 1))],
        out_specs=(pl.BlockSpec((1, MLA_HEADS, 1, HEAD_LANES, tk), lambda bi, i: (bi, 0, i, 0, 0)), row(hl),
                   pl.BlockSpec((1, 1, vr, tk), lambda bi, i: (bi, i, 0, 0))),
        compiler_params=_grid_params(2),
        name="mla_prep",
    )(cqn, ckvn, kr, cost, sint, wqm, wqp, wk, wvt, ones)


ATTN_GENS = 3
ATTN_UNROLL = 4
ATTN_LAG = 2
KEY_SLICE = 128
PV_DEPTH = 256


def _attn_body(qt_ref, k_ref, vt_ref, ot_ref, s_buf, p_buf, bm_buf, a_buf, m_ref, acc_ref, *, nblk):
    nq = qt_ref.shape[2]
    tk = p_buf.shape[1]
    nblocks = nq * nblk
    nslices = tk // KEY_SLICE
    pv_every = PV_DEPTH // KEY_SLICE
    m_ref[...] = jnp.full(m_ref.shape, NEG_BIG, F32)
    acc_ref[...] = jnp.zeros(acc_ref.shape, F32)

    def step(t, phase, do_scores=True, do_softmax=True, do_pv=True):
        g_s, g_m, g_p = phase, (phase + 1) % ATTN_GENS, (phase + 2) % ATTN_GENS
        n_m, n_p = t - ATTN_LAG, t - 2 * ATTN_LAG
        if do_scores:
            j_s = t % nblk
            start = j_s * tk if isinstance(j_s, int) else pl.multiple_of(j_s * tk, tk)
            qt = qt_ref[0, 0, t // nblk]
        if do_softmax:
            m_old = jnp.where(n_m % nblk == 0, NEG_BIG, m_ref[...])
            m_new = jnp.maximum(m_old, bm_buf[g_m])
            a_buf[g_m] = jnp.exp2(m_old - m_new)
            m_ref[...] = m_new
        if do_pv:
            acc = acc_ref[...] * a_buf[g_p]
            vt = vt_ref.at[0, n_p % nblk]
        for c in range(nslices):
            rows = slice(c * KEY_SLICE, (c + 1) * KEY_SLICE)
            if do_pv and c % pv_every == pv_every - 1:
                deep = slice((c + 1 - pv_every) * KEY_SLICE, (c + 1) * KEY_SLICE)
                acc = acc + _dot(vt[:, deep], p_buf[g_p, deep, :])
            if do_softmax:
                p_buf[g_m, rows, :] = jnp.exp2(s_buf[g_m, rows, :] - m_new).astype(BF16)
            if do_scores and c == 0:
                s = _dot(k_ref[pl.ds(start, tk), :], qt)
                s_buf[g_s] = s
                bm_buf[g_s] = jnp.max(s, axis=0, keepdims=True)
        if do_pv:
            acc_ref[...] = acc
            return n_p // nblk, acc
        return None

    def write(out):
        if out is not None:
            ot_ref[0, 0, out[0]] = out[1]

    def static_step(t):
        write(step(t, t % ATTN_GENS, do_scores=t < nblocks, do_softmax=ATTN_LAG <= t < nblocks + ATTN_LAG,
                   do_pv=t >= 2 * ATTN_LAG))

    fill = 2 * ATTN_LAG
    for t in range(fill):
        static_step(t)
    per_iter = ATTN_GENS * ATTN_UNROLL
    nloop = (nblocks - fill) // per_iter

    def body(u, carry):
        t0 = fill + per_iter * u
        outs = [step(t0 + i, (fill + i) % ATTN_GENS) for i in range(per_iter)]
        for out in outs:
            write(out)
        return carry

    lax.fori_loop(0, nloop, body, 0)
    for t in range(fill + per_iter * nloop, nblocks + fill):
        static_step(t)


def _attention(qt, k, vt, *, b, s, tq, tk):
    nq = s // tq
    nk = s // tk
    assert tk % PV_DEPTH == 0 and PV_DEPTH % KEY_SLICE == 0
    return pl.pallas_call(
        functools.partial(_attn_body, nblk=nk),
        out_shape=jax.ShapeDtypeStruct((b, MLA_HEADS, nq, V_ROWS, tq), F32),
        grid=(b, MLA_HEADS),
        in_specs=[pl.BlockSpec((1, 1, nq, HEAD_LANES, tq), lambda bi, h: (bi, h, 0, 0, 0)),
                  pl.BlockSpec((s, HEAD_LANES), lambda bi, h: (bi, h)),
                  pl.BlockSpec((1, nk, V_ROWS, tk), lambda bi, h: (bi, 0, h, 0))],
        out_specs=pl.BlockSpec((1, 1, nq, V_ROWS, tq), lambda bi, h: (bi, h, 0, 0, 0)),
        scratch_shapes=[pltpu.VMEM((ATTN_GENS, tk, tq), F32), pltpu.VMEM((ATTN_GENS, tk, tq), BF16),
                        pltpu.VMEM((ATTN_GENS, 1, tq), F32), pltpu.VMEM((ATTN_GENS, 1, tq), F32),
                        pltpu.VMEM((1, tq), F32), pltpu.VMEM((V_ROWS, tq), F32)],
        compiler_params=_grid_params(2),
        name="attention",
    )(qt, k, vt)


HALO = 8


def _gdn_prep_body(prev_ref, cur_ref, next_ref, cw_ref, q_ref, k_ref, v_ref, buf_ref, *, tiles_per_seq):
    i = pl.program_id(0)
    tm = cur_ref.shape[0]
    first = (i % tiles_per_seq) == 0
    last = (i % tiles_per_seq) == tiles_per_seq - 1
    buf_ref[0:HALO, :] = jnp.where(first, 0.0, prev_ref[...])
    buf_ref[HALO:HALO + tm, :] = cur_ref[...]
    buf_ref[HALO + tm:2 * HALO + tm, :] = jnp.where(last, 0.0, next_ref[...])
    pad = GDN_CONV // 2
    for grp in range(3 * GDN_HEADS):
        lanes = slice(grp * GDN_D, (grp + 1) * GDN_D)
        acc = None
        for j in range(GDN_CONV):
            lo = HALO - pad + j
            term = buf_ref[lo:lo + tm, lanes] * cw_ref[j:j + 1, lanes]
            acc = term if acc is None else acc + term
        y = _silu(acc)
        if grp < 2 * GDN_HEADS:
            y = y * lax.rsqrt(jnp.sum(y * y, axis=-1, keepdims=True) + EPS)
        if grp < GDN_HEADS:
            q_ref[:, lanes] = y * GDN_D ** -0.5
        elif grp < 2 * GDN_HEADS:
            k_ref[:, slice(lanes.start - GDN_W, lanes.stop - GDN_W)] = y
        else:
            v_ref[:, slice(lanes.start - 2 * GDN_W, lanes.stop - 2 * GDN_W)] = y


def _gdn_prep(qkv, conv_w, *, s, tm):
    t, c = qkv.shape
    tiles_per_seq = s // tm
    hb = tm // HALO
    nh = t // HALO
    out = jax.ShapeDtypeStruct((t, GDN_W), F32)
    row = pl.BlockSpec((tm, GDN_W), lambda i: (i, 0))
    return pl.pallas_call(
        functools.partial(_gdn_prep_body, tiles_per_seq=tiles_per_seq),
        out_shape=(out, out, out),
        grid=(t // tm,),
        in_specs=[pl.BlockSpec((HALO, c), lambda i: (jnp.maximum(i * hb - 1, 0), 0)),
                  pl.BlockSpec((tm, c), lambda i: (i, 0)),
                  pl.BlockSpec((HALO, c), lambda i: (jnp.minimum((i + 1) * hb, nh - 1), 0)),
                  _const_spec(conv_w.shape)],
        out_specs=(row, row, row),
        scratch_shapes=[pltpu.VMEM((tm + 2 * HALO, c), F32)],
        compiler_params=_grid_params(1),
        name="gdn_prep",
    )(qkv, qkv, qkv, conv_w)


CH = GDN_CHUNK
HC = GDN_HEADS * CH
SCAN_CHUNKS = 2


def _split3(x):
    hi = x.astype(BF16)
    r = x - hi.astype(F32)
    mid = r.astype(BF16)
    lo = (r - mid.astype(F32)).astype(BF16)
    return hi, mid, lo


def _bcast_cols(cols, width):
    if width == 128:
        return jnp.concatenate([jnp.broadcast_to(c, (CH, 128)) for c in cols], axis=1)
    lane = lax.broadcasted_iota(jnp.int32, (CH, 128), 1)
    lo = lane < 64
    pair = lambda a, b: jnp.where(lo, jnp.broadcast_to(a, (CH, 128)), jnp.broadcast_to(b, (CH, 128)))
    return jnp.concatenate([pair(cols[0], cols[1]), pair(cols[2], cols[3])], axis=1)


def _dir_masks(d):
    r64 = lax.broadcasted_iota(jnp.int32, (CH, CH), 0)
    c64 = lax.broadcasted_iota(jnp.int32, (CH, CH), 1)
    r256 = lax.broadcasted_iota(jnp.int32, (CH, HC), 0)
    c256 = lax.broadcasted_iota(jnp.int32, (CH, HC), 1) & (CH - 1)
    if d == 0:
        tri, incl, strict, trit, last = (c64 <= r64), (c256 <= r256), (c256 < r256), (r256 <= c256), CH - 1
    else:
        tri, incl, strict, trit, last = (c64 >= r64), (c256 >= r256), (c256 > r256), (r256 >= c256), 0
    tri3 = jnp.concatenate([tri.astype(BF16)] * 3, axis=1)
    trit3 = jnp.concatenate([trit.astype(BF16)] * 3, axis=0)
    return dict(tri3=tri3, trit3=trit3, incl=incl, strict=strict, last=last)


def _pair_diag(a, b):
    z = jnp.zeros(a.shape, a.dtype)
    return jnp.concatenate([jnp.concatenate([a, z], axis=1), jnp.concatenate([z, b], axis=1)], axis=0)


def _gdn_scan_body(qf_ref, kf_ref, vf_ref, bgf_ref, gtf_ref, qb_ref, kb_ref, vb_ref, bgb_ref, gtb_ref,
                   of_ref, ob_ref, s_ref):
    @pl.when(pl.program_id(0) == 0)
    def _():
        s_ref[...] = jnp.zeros(s_ref.shape, F32)

    nb = qf_ref.shape[0]
    ncs = qf_ref.shape[1] // CH
    in_refs = ((qf_ref, kf_ref, vf_ref, bgf_ref, gtf_ref), (qb_ref, kb_ref, vb_ref, bgb_ref, gtb_ref))
    out_refs = (of_ref, ob_ref)
    masks = (_dir_masks(0), _dir_masks(1))
    groups = [(d, b, r if d == 0 else ncs - 1 - r) for r in range(ncs) for b in range(nb) for d in (0, 1)]
    per_rank = 2 * nb

    def rows(d, idx, b, c):
        return in_refs[d][idx][b, c * CH:(c + 1) * CH, :]
    heads = range(GDN_HEADS)
    pairs = range(GDN_HEADS // 2)

    r256 = lax.broadcasted_iota(jnp.int32, (CH, HC), 0)
    c256 = lax.broadcasted_iota(jnp.int32, (CH, HC), 1) & (CH - 1)
    eye = (c256 == r256).astype(F32)
    bd_mask = ((lax.broadcasted_iota(jnp.int32, (HC, HC), 0) >> 6)
               == (lax.broadcasted_iota(jnp.int32, (HC, HC), 1) >> 6)).astype(BF16)
    kbd_mask = ((lax.broadcasted_iota(jnp.int32, (HC, GDN_W), 0) >> 6)
                == (lax.broadcasted_iota(jnp.int32, (HC, GDN_W), 1) >> 7)).astype(BF16)
    r8 = lax.broadcasted_iota(jnp.int32, (2 * GDN_HEADS, HC), 0)
    h8 = lax.broadcasted_iota(jnp.int32, (2 * GDN_HEADS, HC), 1) >> 6

    st = []
    for d, b, c in groups:
        mk = masks[d]
        bg = rows(d, 3, b, c)
        gt = in_refs[d][4][b, c]
        cs = _dot(mk["tri3"], jnp.concatenate(_split3(bg), axis=0))
        cst = _dot(jnp.concatenate(_split3(gt), axis=1), mk["trit3"])
        st.append(dict(bg=bg, cs=cs, cst=cst))

    for g, (d, b, c) in enumerate(groups):
        mk, e = masks[d], st[g]
        crow = jnp.sum(jnp.where(r8 == d * GDN_HEADS + h8, e["cst"], 0.0), axis=0, keepdims=True)
        goff = 2 * GDN_HEADS + d * GDN_HEADS
        ccols = [e["cs"][:, goff + h:goff + h + 1] for h in heads]
        betas = [e["bg"][:, d * GDN_HEADS + h:d * GDN_HEADS + h + 1] for h in heads]
        e["ccols"] = ccols
        e["ccol512"] = _bcast_cols(ccols, GDN_D)
        e["beta512"] = _bcast_cols(betas, GDN_D)
        e["decay"] = jnp.exp(jnp.where(mk["incl"], _bcast_cols(ccols, CH) - crow, -jnp.inf))

    for g, (d, b, c) in enumerate(groups):
        mk, e = masks[d], st[g]
        q = rows(d, 0, b, c)
        k = rows(d, 1, b, c)
        kb = k * e["beta512"]
        kbd = jnp.concatenate([k.astype(BF16)] * GDN_HEADS, axis=0) * kbd_mask
        kq = lax.dot_general(jnp.concatenate([kb, q], axis=0).astype(BF16), kbd, NT_DIMS,
                             preferred_element_type=F32)
        neg_l = jnp.where(mk["strict"], -(kq[:CH] * e["decay"]), 0.0)
        e["intra"] = (kq[CH:] * e["decay"]).astype(BF16)
        e["p"] = eye + neg_l
        e["lm"] = neg_l

    for level in range(6):
        for e in st:
            w_bd = jnp.concatenate([e["lm"].astype(BF16)] * GDN_HEADS, axis=0) * bd_mask
            if level == 0:
                e["lm"] = _dot(e["lm"].astype(BF16), w_bd)
            elif level < 5:
                y = _dot(jnp.concatenate([e["p"], e["lm"]], axis=0).astype(BF16), w_bd)
                e["p"] = e["p"] + y[:CH]
                e["lm"] = y[CH:]
            else:
                e["p"] = e["p"] + _dot(e["p"].astype(BF16), w_bd)

    for g, (d, b, c) in enumerate(groups):
        e = st[g]
        k = rows(d, 1, b, c)
        v = rows(d, 2, b, c)
        tinv = e["p"].astype(BF16)
        vb = (v * e["beta512"]).astype(BF16)
        kbe = (k * e["beta512"] * jnp.exp(e["ccol512"])).astype(BF16)
        e["uw"] = []
        for pr in pairs:
            h0, h1 = 2 * pr, 2 * pr + 1
            rhs = lambda h: jnp.concatenate([vb[:, h * GDN_D:(h + 1) * GDN_D], kbe[:, h * GDN_D:(h + 1) * GDN_D]], axis=1)
            e["uw"].append(_dot(tinv[:, pr * GDN_D:(pr + 1) * GDN_D], _pair_diag(rhs(h0), rhs(h1))))

    for r in range(ncs):
        _gdn_state_stage(st, groups, r * per_rank, (r + 1) * per_rank, rows, out_refs, masks, s_ref)


def _gdn_state_stage(st, groups, lo, hi, rows, out_refs, masks, s_ref):
    pairs = range(GDN_HEADS // 2)
    for g in range(lo, hi):
        d, b, c = groups[g]
        e = st[g]
        q = rows(d, 0, b, c)
        qe = q * jnp.exp(e["ccol512"])
        e["ws"] = []
        for pr in pairs:
            lhs = []
            for h in (2 * pr, 2 * pr + 1):
                w = e["uw"][pr][:, (2 * (h % 2) + 1) * GDN_D:(2 * (h % 2) + 2) * GDN_D]
                lhs.append(jnp.concatenate([w, qe[:, h * GDN_D:(h + 1) * GDN_D]], axis=0))
            sidx = (b * 2 + d) * GDN_HEADS + 2 * pr
            sbd = _pair_diag(s_ref[sidx].astype(BF16), s_ref[sidx + 1].astype(BF16))
            e["ws"].append(_dot(jnp.concatenate(lhs, axis=1).astype(BF16), sbd))

    for g in range(lo, hi):
        d, b, c = groups[g]
        e = st[g]
        k = rows(d, 1, b, c)
        last = masks[d]["last"]
        for pr in pairs:
            vnew = []
            for h in (2 * pr, 2 * pr + 1):
                u = e["uw"][pr][:, 2 * (h % 2) * GDN_D:(2 * (h % 2) + 1) * GDN_D]
                vnew.append((u - e["ws"][pr][:CH, (h % 2) * GDN_D:(h % 2 + 1) * GDN_D]).astype(BF16))
            glast = [e["ccols"][h][last:last + 1, :] for h in (2 * pr, 2 * pr + 1)]
            kd = [k[:, h * GDN_D:(h + 1) * GDN_D] * jnp.exp(gl - e["ccol512"][:, h * GDN_D:(h + 1) * GDN_D])
                  for h, gl in zip((2 * pr, 2 * pr + 1), glast)]
            kdt = jnp.concatenate(kd, axis=0).T.astype(BF16)
            lhs = jnp.concatenate([e["intra"][:, pr * GDN_D:(pr + 1) * GDN_D], kdt], axis=0)
            prod = _dot(lhs, _pair_diag(vnew[0], vnew[1]))
            out_refs[d][b, c * CH:(c + 1) * CH, 2 * pr * GDN_D:(2 * pr + 2) * GDN_D] = e["ws"][pr][CH:, :] + prod[:CH]
            for i, h in enumerate((2 * pr, 2 * pr + 1)):
                sidx = (b * 2 + d) * GDN_HEADS + h
                s_ref[sidx] = s_ref[sidx] * jnp.exp(glast[i]) + prod[CH:, i * GDN_D:(i + 1) * GDN_D]


def _gdn_scan(q, k, v, bg, gt, *, b, s):
    ncs = SCAN_CHUNKS if (s // CH) % SCAN_CHUNKS == 0 else 1
    n = s // (CH * ncs)
    fwd = lambda i: (0, i, 0)
    bwd = lambda i: (0, n - 1 - i, 0)
    fwd_t = lambda i: (0, i, 0, 0)
    bwd_t = lambda i: (0, n - 1 - i, 0, 0)
    wide = lambda im: pl.BlockSpec((b, ncs * CH, GDN_W), im)
    specs = lambda im, imt: [wide(im), wide(im), wide(im), pl.BlockSpec((b, ncs * CH, N_BA), im),
                             pl.BlockSpec((b, ncs, 2 * GDN_HEADS, CH), imt)]
    out = jax.ShapeDtypeStruct((b, s, GDN_W), F32)
    return pl.pallas_call(
        _gdn_scan_body,
        out_shape=(out, out),
        grid=(n,),
        in_specs=specs(fwd, fwd_t) + specs(bwd, bwd_t),
        out_specs=(wide(fwd), wide(bwd)),
        scratch_shapes=[pltpu.VMEM((b * 2 * GDN_HEADS, GDN_D, GDN_D), F32)],
        compiler_params=_grid_params(1),
        name="gdn_scan",
    )(q, k, v, bg, gt, q, k, v, bg, gt)


def _merge_body(x_ref, of_ref, ob_ref, sz_ref, sg_ref, ot_ref, gnw_ref, wgp_ref, wmp_ref, wo_ref, o_ref):
    o = of_ref[...] + ob_ref[...]
    gnw = gnw_ref[...]
    heads = []
    for h in range(GDN_HEADS):
        oh = o[:, h * GDN_D:(h + 1) * GDN_D]
        heads.append(_rms(oh, gnw))
    on = jnp.concatenate(heads, axis=1) * sz_ref[...]
    ya = _dot(on.astype(BF16), wgp_ref[...])
    acc_t = ot_ref[0, :, 0]
    ot = (acc_t[:, :MLA_V, :] / acc_t[:, MLA_V:MLA_V + 1, :]).reshape(MLA_HEADS * MLA_V, -1)
    yb = _dot(ot.T.astype(BF16), wmp_ref[...])
    d = ya.shape[1]
    y = sg_ref[:, :d] * ya + sg_ref[:, d:] * yb
    o_ref[...] = x_ref[...] + _dot(y.astype(BF16), wo_ref[...])


def _merge(x, of, ob, sz, sg, ot, gnw, wgp, wmp, wo, *, b, s, tm):
    t, d = x.shape
    nsteps = s // tm
    row = lambda n: pl.BlockSpec((tm, n), lambda bi, i: (bi * nsteps + i, 0))
    return pl.pallas_call(
        _merge_body,
        out_shape=jax.ShapeDtypeStruct((t, d), F32),
        grid=(b, nsteps),
        in_specs=[row(d), row(GDN_W), row(GDN_W), row(GDN_W), row(2 * d),
                  pl.BlockSpec((1, MLA_HEADS, 1, V_ROWS, tm), lambda bi, i: (bi, 0, i, 0, 0)),
                  _const_spec((1, GDN_D)), _const_spec(wgp.shape), _const_spec(wmp.shape), _const_spec(wo.shape)],
        out_specs=row(d),
        compiler_params=_grid_params(2),
        name="merge",
    )(x, of, ob, sz, sg, ot, gnw, wgp, wmp, wo)


def _pack_w_in(w):
    d = w.shape[0]
    zeros = lambda n: jnp.zeros((d, n), w.dtype)
    o = 4 * GDN_W
    ba = w[:, o:o + N_BA]
    o += N_BA
    cq = w[:, o:o + MLA_Q_LORA]
    o += MLA_Q_LORA
    ckv = w[:, o:o + MLA_KV_LORA]
    o += MLA_KV_LORA
    kr = w[:, o:o + MLA_ROPE]
    o += MLA_ROPE
    gates = w[:, o:]
    half = MLA_ROPE // 2
    tail = zeros(HEAD_LANES - MLA_NOPE - MLA_ROPE)
    kr_main = jnp.concatenate([zeros(MLA_NOPE), kr, tail], axis=1)
    kr_swap = jnp.concatenate([zeros(MLA_NOPE), kr[:, half:], kr[:, :half], tail], axis=1)
    packed = jnp.concatenate([w[:, :4 * GDN_W], ba, zeros(128 - N_BA), cq, ckv, kr_main, kr_swap, gates], axis=1)
    return packed.astype(BF16), ba.T.astype(BF16)


def _pack_w_uq(w):
    r = w.shape[0]
    qk = MLA_NOPE + MLA_ROPE
    half = MLA_ROPE // 2
    tail = jnp.zeros((r, HEAD_LANES - qk), w.dtype)
    znope = jnp.zeros((r, MLA_NOPE), w.dtype)
    main, swap = [], []
    for h in range(MLA_HEADS):
        nope = w[:, h * qk:h * qk + MLA_NOPE]
        rope = w[:, h * qk + MLA_NOPE:(h + 1) * qk]
        main += [nope, rope, tail]
        swap += [znope, rope[:, half:], rope[:, :half], tail]
    return jnp.concatenate(main, axis=1).T.astype(BF16), jnp.concatenate(swap, axis=1).T.astype(BF16)


def _pack_w_ukv(w):
    r = w.shape[0]
    hw = MLA_NOPE + MLA_V
    ks, vs = [], []
    for h in range(MLA_HEADS):
        ks += [w[:, h * hw:h * hw + MLA_NOPE], jnp.zeros((r, HEAD_LANES - MLA_NOPE), w.dtype)]
        vs += [w[:, h * hw + MLA_NOPE:(h + 1) * hw], jnp.zeros((r, V_ROWS - MLA_V), w.dtype)]
    return jnp.concatenate(ks, axis=1).astype(BF16), jnp.concatenate(vs, axis=1).T.astype(BF16)


def _ones_rows():
    idx = jnp.arange(MLA_HEADS * V_ROWS) % V_ROWS
    return (idx == MLA_V).astype(F32)[:, None]


def _lane_pad(v, lo, width):
    return jnp.zeros((1, width), v.dtype).at[0, lo:lo + v.shape[0]].set(v)


def kernel(x, positions, norm_ffn1, ffn1_w_gate, ffn1_w_up, ffn1_w_down, norm_mix, w_in, gdn_conv, gdn_A_log,
           gdn_dt_bias, gdn_norm, gdn_proj, mla_q_norm, mla_w_uq, mla_kv_norm, mla_w_ukv, mla_proj, w_out,
           norm_ffn2, ffn2_w_gate, ffn2_w_up, ffn2_w_down, final_norm):
    b, s, d = x.shape
    t = b * s
    depth = w_in.shape[0]
    tm = 512 if s % 2048 == 0 else 256
    ffn_tm = 512
    tq = tk = tm

    cos, sin, cost, sint = _rope_tables(positions, tm)
    ones = _ones_rows()
    xf = x.reshape(t, d)
    row = lambda v: v.reshape(1, -1)
    for l in range(depth):
        xf = _ffn(xf, row(norm_ffn1[l]), ffn1_w_gate[l].astype(BF16), ffn1_w_up[l].astype(BF16),
                  ffn1_w_down[l].astype(BF16), row(final_norm), final_norm=False, tm=ffn_tm)

        w_packed, w_bat = _pack_w_in(w_in[l])
        alog = gdn_A_log[l].reshape(-1)
        dtb = gdn_dt_bias[l].reshape(-1)
        qkv, sz, bg, gt, cqn, ckvn, kr, sg = _inproj(
            xf, row(norm_mix[l]), w_packed, w_bat, _lane_pad(alog, 2 * GDN_HEADS, 128),
            _lane_pad(dtb, 2 * GDN_HEADS, 128), alog[:, None], dtb[:, None], row(mla_q_norm[l]),
            row(mla_kv_norm[l]), cos, sin, tm=tm)

        qn, kn, vv = _gdn_prep(qkv, gdn_conv[l], s=s, tm=tm)
        seq = lambda a: a.reshape(b, s, a.shape[-1])
        gt_chunks = gt.reshape(2 * GDN_HEADS, b, s // CH, CH).transpose(1, 2, 0, 3)
        of, ob = _gdn_scan(seq(qn), seq(kn), seq(vv), seq(bg), gt_chunks, b=b, s=s)
        of, ob = of.reshape(t, GDN_W), ob.reshape(t, GDN_W)

        wqm, wqp = _pack_w_uq(mla_w_uq[l])
        wk, wvt = _pack_w_ukv(mla_w_ukv[l])
        qt, kk, vt = _mla_prep(cqn, ckvn, kr, cost, sint, wqm, wqp, wk, wvt, ones, b=b, s=s, tk=tk)
        ot = _attention(qt, kk, vt, b=b, s=s, tq=tq, tk=tk)

        xf = _merge(xf, of, ob, sz, sg, ot, row(gdn_norm[l]), gdn_proj[l].astype(BF16), mla_proj[l].astype(BF16),
                    w_out[l].astype(BF16), b=b, s=s, tm=tm)

        xf = _ffn(xf, row(norm_ffn2[l]), ffn2_w_gate[l].astype(BF16), ffn2_w_up[l].astype(BF16),
                  ffn2_w_down[l].astype(BF16), row(final_norm), final_norm=(l == depth - 1), tm=ffn_tm)
    return xf.reshape(b, s, d)
```

```python
import functools

import jax
import jax.numpy as jnp
from jax import lax
from jax.experimental import pallas as pl
from jax.experimental.pallas import tpu as pltpu

F32 = jnp.float32
BF16 = jnp.bfloat16

EPS = 1e-6
RES_HALF = 0.5
GDN_HEADS = 4
GDN_D = 128
GDN_CONV = 5
GDN_CHUNK = 64
MLA_HEADS = 8
MLA_NOPE = 64
MLA_ROPE = 32
MLA_V = 64
MLA_Q_LORA = 384
MLA_KV_LORA = 256
ROPE_THETA = 10000.0
HEAD_LANES = 128
V_ROWS = 80
LOG2E = 1.4426950408889634
NEG_BIG = -1e30

VMEM_LIMIT_BYTES = 56 * 1024 * 1024

NT_DIMS = (((1,), (1,)), ((), ()))


def _grid_params(n, flags=None):
    return pltpu.CompilerParams(dimension_semantics=("arbitrary",) * n, vmem_limit_bytes=VMEM_LIMIT_BYTES,
                                flags=flags)


def _const_spec(shape):
    nd = len(shape)
    return pl.BlockSpec(shape, lambda *_: (0,) * nd, pipeline_mode=pl.Buffered(1))


def _rms(x, w):
    return x * lax.rsqrt(jnp.mean(x * x, axis=-1, keepdims=True) + EPS) * w


def _silu(x):
    return x * jax.nn.sigmoid(x)


def _softplus(x):
    return jnp.maximum(x, 0.0) + jnp.log1p(jnp.exp(-jnp.abs(x)))


def _dot(a, b):
    return jnp.dot(a, b, preferred_element_type=F32)


def _ffn_body(x_ref, nw_ref, wg_ref, wu_ref, wd_ref, fw_ref, o_ref, *, final_norm):
    x = x_ref[...]
    hb = _rms(x, nw_ref[...]).astype(BF16)
    g = _dot(hb, wg_ref[...])
    u = _dot(hb, wu_ref[...])
    a = (_silu(g) * u).astype(BF16)
    y = x + RES_HALF * _dot(a, wd_ref[...])
    if final_norm:
        y = _rms(y, fw_ref[...])
    o_ref[...] = y


def _ffn(x, nw, wg, wu, wd, fw, *, final_norm, tm):
    t, d = x.shape
    ff = wg.shape[1]
    row = pl.BlockSpec((tm, d), lambda i: (i, 0))
    return pl.pallas_call(
        functools.partial(_ffn_body, final_norm=final_norm),
        out_shape=jax.ShapeDtypeStruct((t, d), F32),
        grid=(t // tm,),
        in_specs=[row, _const_spec((1, d)), _const_spec((d, ff)), _const_spec((d, ff)),
                  _const_spec((ff, d)), _const_spec((1, d))],
        out_specs=row,
        compiler_params=_grid_params(1),
        name="ffn",
    )(x, nw, wg, wu, wd, fw)


def _rope_body(post_ref, fcol_ref, cos_ref, sin_ref, cost_ref, sint_ref):
    ang = fcol_ref[...] * post_ref[0].astype(F32)
    c = jnp.cos(ang)
    s = jnp.sin(ang)
    tm = ang.shape[1]
    pad = HEAD_LANES - MLA_NOPE - MLA_ROPE
    cost = jnp.concatenate([jnp.ones((MLA_NOPE, tm), F32), c, c, jnp.ones((pad, tm), F32)], axis=0)
    sint = jnp.concatenate([jnp.zeros((MLA_NOPE, tm), F32), -s, s, jnp.zeros((pad, tm), F32)], axis=0)
    cost_ref[0] = cost
    sint_ref[0] = sint
    cos_ref[...] = cost.T
    sin_ref[...] = sint.T


def _rope_tables(positions, tm):
    b, s = positions.shape
    t = b * s
    inv_freq = jnp.power(ROPE_THETA, -jnp.arange(0, MLA_ROPE, 2, dtype=F32) / MLA_ROPE)
    nsteps = s // tm
    return pl.pallas_call(
        _rope_body,
        out_shape=(jax.ShapeDtypeStruct((t, HEAD_LANES), F32), jax.ShapeDtypeStruct((t, HEAD_LANES), F32),
                   jax.ShapeDtypeStruct((b, HEAD_LANES, s), F32), jax.ShapeDtypeStruct((b, HEAD_LANES, s), F32)),
        grid=(b, nsteps),
        in_specs=[pl.BlockSpec((1, 1, tm), lambda bi, i: (bi, 0, i)), _const_spec((MLA_ROPE // 2, 1))],
        out_specs=(pl.BlockSpec((tm, HEAD_LANES), lambda bi, i: (bi * nsteps + i, 0)),
                   pl.BlockSpec((tm, HEAD_LANES), lambda bi, i: (bi * nsteps + i, 0)),
                   pl.BlockSpec((1, HEAD_LANES, tm), lambda bi, i: (bi, 0, i)),
                   pl.BlockSpec((1, HEAD_LANES, tm), lambda bi, i: (bi, 0, i))),
        compiler_params=_grid_params(2),
        name="rope_tables",
    )(positions.reshape(b, 1, s), inv_freq[:, None])


GDN_W = GDN_HEADS * GDN_D
SEG_QKV = (0, 3 * GDN_W)
SEG_Z = (SEG_QKV[1], SEG_QKV[1] + GDN_W)
SEG_BA = (SEG_Z[1], SEG_Z[1] + 128)
SEG_CQ = (SEG_BA[1], SEG_BA[1] + MLA_Q_LORA)
SEG_CKV = (SEG_CQ[1], SEG_CQ[1] + MLA_KV_LORA)
SEG_KRM = (SEG_CKV[1], SEG_CKV[1] + HEAD_LANES)
SEG_KRP = (SEG_KRM[1], SEG_KRM[1] + HEAD_LANES)
SEG_GATE = (SEG_KRP[1], SEG_KRP[1] + 2048)
N_BA = 4 * GDN_HEADS


def _inproj_body(x_ref, nw_ref, w_ref, wbat_ref, alog_ref, dtb_ref, alogt_ref, dtbt_ref, qnw_ref, kvnw_ref,
                 cos_ref, sin_ref, qkv_ref, sz_ref, bg_ref, gt_ref, cqn_ref, ckvn_ref, kr_ref, sg_ref):
    hb = _rms(x_ref[...], nw_ref[...]).astype(BF16)

    def seg(bounds):
        return _dot(hb, w_ref[:, bounds[0]:bounds[1]])

    qkv_ref[...] = seg(SEG_QKV)
    sz_ref[...] = _silu(seg(SEG_Z)).astype(BF16)
    ba = seg(SEG_BA)
    lane = lax.broadcasted_iota(jnp.int32, ba.shape, 1)
    decay = -jnp.exp(alog_ref[...]) * _softplus(ba + dtb_ref[...])
    bg_ref[...] = jnp.where(lane < 2 * GDN_HEADS, jax.nn.sigmoid(ba), decay)[:, :N_BA]
    bat = lax.dot_general(wbat_ref[...], hb, NT_DIMS, preferred_element_type=F32)
    at = bat[2 * GDN_HEADS:, :]
    gt_ref[...] = -jnp.exp(alogt_ref[...]) * _softplus(at + dtbt_ref[...])
    cqn_ref[...] = _rms(seg(SEG_CQ), qnw_ref[...]).astype(BF16)
    ckvn_ref[...] = _rms(seg(SEG_CKV), kvnw_ref[...]).astype(BF16)
    kr_ref[...] = seg(SEG_KRM) * cos_ref[...] + seg(SEG_KRP) * sin_ref[...]
    sg_ref[...] = jax.nn.sigmoid(seg(SEG_GATE)).astype(BF16)


def _inproj(x, nw, w, wbat, alog, dtb, alogt, dtbt, qnw, kvnw, cos, sin, *, tm):
    t, d = x.shape
    row = lambda n: pl.BlockSpec((tm, n), lambda i: (i, 0))
    out_shape = (
        jax.ShapeDtypeStruct((t, 3 * GDN_W), F32),
        jax.ShapeDtypeStruct((t, GDN_W), BF16),
        jax.ShapeDtypeStruct((t, N_BA), F32),
        jax.ShapeDtypeStruct((2 * GDN_HEADS, t), F32),
        jax.ShapeDtypeStruct((t, MLA_Q_LORA), BF16),
        jax.ShapeDtypeStruct((t, MLA_KV_LORA), BF16),
        jax.ShapeDtypeStruct((t, HEAD_LANES), F32),
        jax.ShapeDtypeStruct((t, 2048), BF16),
    )
    out_specs = (row(3 * GDN_W), row(GDN_W), row(N_BA), pl.BlockSpec((2 * GDN_HEADS, tm), lambda i: (0, i)),
                 row(MLA_Q_LORA), row(MLA_KV_LORA), row(HEAD_LANES), row(2048))
    return pl.pallas_call(
        _inproj_body,
        out_shape=out_shape,
        grid=(t // tm,),
        in_specs=[row(d), _const_spec((1, d)), _const_spec(w.shape), _const_spec(wbat.shape),
                  _const_spec((1, 128)), _const_spec((1, 128)),
                  _const_spec((2 * GDN_HEADS, 1)), _const_spec((2 * GDN_HEADS, 1)),
                  _const_spec((1, MLA_Q_LORA)), _const_spec((1, MLA_KV_LORA)),
                  row(HEAD_LANES), row(HEAD_LANES)],
        out_specs=out_specs,
        compiler_params=_grid_params(1),
        name="inproj",
    )(x, nw, w, wbat, alog, dtb, alogt, dtbt, qnw, kvnw, cos, sin)


def _mla_prep_body(cqn_ref, ckvn_ref, kr_ref, cost_ref, sint_ref, wqm_ref, wqp_ref, wk_ref, wvt_ref, ones_ref,
                   qt_ref, k_ref, vt_ref):
    cqn = cqn_ref[...]
    qm = lax.dot_general(wqm_ref[...], cqn, NT_DIMS, preferred_element_type=F32)
    qp = lax.dot_general(wqp_ref[...], cqn, NT_DIMS, preferred_element_type=F32)
    qscale = (MLA_NOPE + MLA_ROPE) ** -0.5 * LOG2E
    cost = cost_ref[0] * qscale
    sint = sint_ref[0] * qscale
    ckvn = ckvn_ref[...]
    km = _dot(ckvn, wk_ref[...])
    kr = kr_ref[...]
    for h in range(MLA_HEADS):
        grp = slice(h * HEAD_LANES, (h + 1) * HEAD_LANES)
        qt_ref[0, h, 0] = (qm[grp, :] * cost + qp[grp, :] * sint).astype(BF16)
        k_ref[:, grp] = (km[:, grp] + kr).astype(BF16)
    vt = lax.dot_general(wvt_ref[...], ckvn, NT_DIMS, preferred_element_type=F32)
    vt_ref[0, 0] = (vt + ones_ref[...]).astype(BF16)


def _mla_prep(cqn, ckvn, kr, cost, sint, wqm, wqp, wk, wvt, ones, *, b, s, tk):
    t = b * s
    nsteps = s // tk
    hl = MLA_HEADS * HEAD_LANES
    vr = MLA_HEADS * V_ROWS
    row = lambda n: pl.BlockSpec((tk, n), lambda bi, i: (bi * nsteps + i, 0))
    tr = pl.BlockSpec((1, HEAD_LANES, tk), lambda bi, i: (bi, 0, i))
    return pl.pallas_call(
        _mla_prep_body,
        out_shape=(jax.ShapeDtypeStruct((b, MLA_HEADS, nsteps, HEAD_LANES, tk), BF16),
                   jax.ShapeDtypeStruct((t, hl), BF16),
                   jax.ShapeDtypeStruct((b, nsteps, vr, tk), BF16)),
        grid=(b, nsteps),
        in_specs=[row(MLA_Q_LORA), row(MLA_KV_LORA), row(HEAD_LANES), tr, tr,
                  _const_spec(wqm.shape), _const_spec(wqp.shape), _const_spec(wk.shape), _const_spec(wvt.shape),
                  _const_spec((vr, 1))],
        out_specs=(pl.BlockSpec((1, MLA_HEADS, 1, HEAD_LANES, tk), lambda bi, i: (bi, 0, i, 0, 0)), row(hl),
                   pl.BlockSpec((1, 1, vr, tk), lambda bi, i: (bi, i, 0, 0))),
        compiler_params=_grid_params(2),
        name="mla_prep",
    )(cqn, ckvn, kr, cost, sint, wqm, wqp, wk, wvt, ones)


ATTN_GENS = 3
ATTN_UNROLL = 3
ATTN_LAG = 2
KEY_SLICE = 128
PV_DEPTH = 256


def _attn_body(qt_ref, k_ref, vt_ref, ot_ref, s_buf, p_buf, bm_buf, a_buf, m_ref, acc_ref, *, nblk):
    nq = qt_ref.shape[2]
    tk = p_buf.shape[1]
    nblocks = nq * nblk
    nslices = tk // KEY_SLICE
    pv_every = PV_DEPTH // KEY_SLICE
    m_ref[...] = jnp.full(m_ref.shape, NEG_BIG, F32)
    acc_ref[...] = jnp.zeros(acc_ref.shape, F32)

    def step(t, phase, do_scores=True, do_softmax=True, do_pv=True):
        g_s, g_m, g_p = phase, (phase + 1) % ATTN_GENS, (phase + 2) % ATTN_GENS
        n_m, n_p = t - ATTN_LAG, t - 2 * ATTN_LAG
        if do_scores:
            j_s = t % nblk
            start = j_s * tk if isinstance(j_s, int) else pl.multiple_of(j_s * tk, tk)
            qt = qt_ref[0, 0, t // nblk]
        if do_softmax:
            m_old = jnp.where(n_m % nblk == 0, NEG_BIG, m_ref[...])
            m_new = jnp.maximum(m_old, bm_buf[g_m])
            a_buf[g_m] = jnp.exp2(m_old - m_new)
            m_ref[...] = m_new
        if do_pv:
            acc = acc_ref[...] * a_buf[g_p]
            vt = vt_ref.at[0, n_p % nblk]
        for c in range(nslices):
            rows = slice(c * KEY_SLICE, (c + 1) * KEY_SLICE)
            if do_pv and c % pv_every == pv_every - 1:
                deep = slice((c + 1 - pv_every) * KEY_SLICE, (c + 1) * KEY_SLICE)
                acc = acc + _dot(vt[:, deep], p_buf[g_p, deep, :])
            if do_softmax:
                p_buf[g_m, rows, :] = jnp.exp2(s_buf[g_m, rows, :] - m_new).astype(BF16)
            if do_scores and c == 0:
                s = _dot(k_ref[pl.ds(start, tk), :], qt)
                s_buf[g_s] = s
                bm_buf[g_s] = jnp.max(s, axis=0, keepdims=True)
        if do_pv:
            acc_ref[...] = acc
            return n_p // nblk, acc
        return None

    def write(out):
        if out is not None:
            ot_ref[0, 0, out[0]] = out[1].astype(BF16)

    def static_step(t):
        write(step(t, t % ATTN_GENS, do_scores=t < nblocks, do_softmax=ATTN_LAG <= t < nblocks + ATTN_LAG,
                   do_pv=t >= 2 * ATTN_LAG))

    fill = 2 * ATTN_LAG
    for t in range(fill):
        static_step(t)
    per_iter = ATTN_GENS * ATTN_UNROLL
    nloop = (nblocks - fill) // per_iter

    def body(u, carry):
        t0 = fill + per_iter * u
        outs = [step(t0 + i, (fill + i) % ATTN_GENS) for i in range(per_iter)]
        for out in outs:
            write(out)
        return carry

    lax.fori_loop(0, nloop, body, 0)
    for t in range(fill + per_iter * nloop, nblocks + fill):
        static_step(t)


def _attention(qt, k, vt, *, b, s, tq, tk):
    nq = s // tq
    nk = s // tk
    assert tk % PV_DEPTH == 0 and PV_DEPTH % KEY_SLICE == 0
    return pl.pallas_call(
        functools.partial(_attn_body, nblk=nk),
        out_shape=jax.ShapeDtypeStruct((b, MLA_HEADS, nq, V_ROWS, tq), BF16),
        grid=(b, MLA_HEADS),
        in_specs=[pl.BlockSpec((1, 1, nq, HEAD_LANES, tq), lambda bi, h: (bi, h, 0, 0, 0)),
                  pl.BlockSpec((s, HEAD_LANES), lambda bi, h: (bi, h)),
                  pl.BlockSpec((1, nk, V_ROWS, tk), lambda bi, h: (bi, 0, h, 0))],
        out_specs=pl.BlockSpec((1, 1, nq, V_ROWS, tq), lambda bi, h: (bi, h, 0, 0, 0)),
        scratch_shapes=[pltpu.VMEM((ATTN_GENS, tk, tq), F32), pltpu.VMEM((ATTN_GENS, tk, tq), BF16),
                        pltpu.VMEM((ATTN_GENS, 1, tq), F32), pltpu.VMEM((ATTN_GENS, 1, tq), F32),
                        pltpu.VMEM((1, tq), F32), pltpu.VMEM((V_ROWS, tq), F32)],
        compiler_params=_grid_params(2),
        name="attention",
    )(qt, k, vt)


HALO = 8


def _gdn_prep_body(prev_ref, cur_ref, next_ref, cw_ref, q_ref, k_ref, v_ref, buf_ref, *, tiles_per_seq):
    i = pl.program_id(0)
    tm = cur_ref.shape[0]
    first = (i % tiles_per_seq) == 0
    last = (i % tiles_per_seq) == tiles_per_seq - 1
    buf_ref[0:HALO, :] = jnp.where(first, 0.0, prev_ref[...])
    buf_ref[HALO:HALO + tm, :] = cur_ref[...]
    buf_ref[HALO + tm:2 * HALO + tm, :] = jnp.where(last, 0.0, next_ref[...])
    pad = GDN_CONV // 2
    for grp in range(3 * GDN_HEADS):
        lanes = slice(grp * GDN_D, (grp + 1) * GDN_D)
        acc = None
        for j in range(GDN_CONV):
            lo = HALO - pad + j
            term = buf_ref[lo:lo + tm, lanes] * cw_ref[j:j + 1, lanes]
            acc = term if acc is None else acc + term
        y = _silu(acc)
        if grp < 2 * GDN_HEADS:
            y = y * lax.rsqrt(jnp.sum(y * y, axis=-1, keepdims=True) + EPS)
        if grp < GDN_HEADS:
            q_ref[:, lanes] = y * GDN_D ** -0.5
        elif grp < 2 * GDN_HEADS:
            k_ref[:, slice(lanes.start - GDN_W, lanes.stop - GDN_W)] = y
        else:
            v_ref[:, slice(lanes.start - 2 * GDN_W, lanes.stop - 2 * GDN_W)] = y


def _gdn_prep(qkv, conv_w, *, s, tm):
    t, c = qkv.shape
    tiles_per_seq = s // tm
    hb = tm // HALO
    nh = t // HALO
    out = jax.ShapeDtypeStruct((t, GDN_W), F32)
    row = pl.BlockSpec((tm, GDN_W), lambda i: (i, 0))
    return pl.pallas_call(
        functools.partial(_gdn_prep_body, tiles_per_seq=tiles_per_seq),
        out_shape=(out, out, out),
        grid=(t // tm,),
        in_specs=[pl.BlockSpec((HALO, c), lambda i: (jnp.maximum(i * hb - 1, 0), 0)),
                  pl.BlockSpec((tm, c), lambda i: (i, 0)),
                  pl.BlockSpec((HALO, c), lambda i: (jnp.minimum((i + 1) * hb, nh - 1), 0)),
                  _const_spec(conv_w.shape)],
        out_specs=(row, row, row),
        scratch_shapes=[pltpu.VMEM((tm + 2 * HALO, c), F32)],
        compiler_params=_grid_params(1),
        name="gdn_prep",
    )(qkv, qkv, qkv, conv_w)


CH = GDN_CHUNK
HC = GDN_HEADS * CH
SCAN_CHUNKS = 2


def _split3(x):
    hi = x.astype(BF16)
    r = x - hi.astype(F32)
    mid = r.astype(BF16)
    lo = (r - mid.astype(F32)).astype(BF16)
    return hi, mid, lo


def _bcast_cols(cols, width):
    if width == 128:
        return jnp.concatenate([jnp.broadcast_to(c, (CH, 128)) for c in cols], axis=1)
    lane = lax.broadcasted_iota(jnp.int32, (CH, 128), 1)
    lo = lane < 64
    pair = lambda a, b: jnp.where(lo, jnp.broadcast_to(a, (CH, 128)), jnp.broadcast_to(b, (CH, 128)))
    return jnp.concatenate([pair(cols[0], cols[1]), pair(cols[2], cols[3])], axis=1)


def _dir_masks(d):
    r64 = lax.broadcasted_iota(jnp.int32, (CH, CH), 0)
    c64 = lax.broadcasted_iota(jnp.int32, (CH, CH), 1)
    r256 = lax.broadcasted_iota(jnp.int32, (CH, HC), 0)
    c256 = lax.broadcasted_iota(jnp.int32, (CH, HC), 1) & (CH - 1)
    if d == 0:
        tri, incl, strict, trit, last = (c64 <= r64), (c256 <= r256), (c256 < r256), (r256 <= c256), CH - 1
    else:
        tri, incl, strict, trit, last = (c64 >= r64), (c256 >= r256), (c256 > r256), (r256 >= c256), 0
    tri3 = jnp.concatenate([tri.astype(BF16)] * 3, axis=1)
    trit3 = jnp.concatenate([trit.astype(BF16)] * 3, axis=0)
    return dict(tri3=tri3, trit3=trit3, incl=incl, strict=strict, last=last)


def _pair_diag(a, b):
    z = jnp.zeros(a.shape, a.dtype)
    return jnp.concatenate([jnp.concatenate([a, z], axis=1), jnp.concatenate([z, b], axis=1)], axis=0)


def _gdn_scan_body(qf_ref, kf_ref, vf_ref, bgf_ref, gtf_ref, qb_ref, kb_ref, vb_ref, bgb_ref, gtb_ref,
                   of_ref, ob_ref, s_ref):
    @pl.when(pl.program_id(0) == 0)
    def _():
        s_ref[...] = jnp.zeros(s_ref.shape, F32)

    nb = qf_ref.shape[0]
    ncs = qf_ref.shape[1] // CH
    in_refs = ((qf_ref, kf_ref, vf_ref, bgf_ref, gtf_ref), (qb_ref, kb_ref, vb_ref, bgb_ref, gtb_ref))
    out_refs = (of_ref, ob_ref)
    masks = (_dir_masks(0), _dir_masks(1))
    groups = [(d, b, r if d == 0 else ncs - 1 - r) for r in range(ncs) for b in range(nb) for d in (0, 1)]
    per_rank = 2 * nb

    def rows(d, idx, b, c):
        return in_refs[d][idx][b, c * CH:(c + 1) * CH, :]
    heads = range(GDN_HEADS)
    pairs = range(GDN_HEADS // 2)

    r256 = lax.broadcasted_iota(jnp.int32, (CH, HC), 0)
    c256 = lax.broadcasted_iota(jnp.int32, (CH, HC), 1) & (CH - 1)
    eye = (c256 == r256).astype(F32)
    bd_mask = ((lax.broadcasted_iota(jnp.int32, (HC, HC), 0) >> 6)
               == (lax.broadcasted_iota(jnp.int32, (HC, HC), 1) >> 6)).astype(BF16)
    kbd_mask = ((lax.broadcasted_iota(jnp.int32, (HC, GDN_W), 0) >> 6)
                == (lax.broadcasted_iota(jnp.int32, (HC, GDN_W), 1) >> 7)).astype(BF16)
    r8 = lax.broadcasted_iota(jnp.int32, (2 * GDN_HEADS, HC), 0)
    h8 = lax.broadcasted_iota(jnp.int32, (2 * GDN_HEADS, HC), 1) >> 6

    st = []
    for d, b, c in groups:
        mk = masks[d]
        bg = rows(d, 3, b, c)
        gt = in_refs[d][4][b, c]
        cs = _dot(mk["tri3"], jnp.concatenate(_split3(bg), axis=0))
        cst = _dot(jnp.concatenate(_split3(gt), axis=1), mk["trit3"])
        st.append(dict(bg=bg, cs=cs, cst=cst))

    for g, (d, b, c) in enumerate(groups):
        mk, e = masks[d], st[g]
        crow = jnp.sum(jnp.where(r8 == d * GDN_HEADS + h8, e["cst"], 0.0), axis=0, keepdims=True)
        goff = 2 * GDN_HEADS + d * GDN_HEADS
        ccols = [e["cs"][:, goff + h:goff + h + 1] for h in heads]
        betas = [e["bg"][:, d * GDN_HEADS + h:d * GDN_HEADS + h + 1] for h in heads]
        e["ccols"] = ccols
        e["ccol512"] = _bcast_cols(ccols, GDN_D)
        e["beta512"] = _bcast_cols(betas, GDN_D)
        e["decay"] = jnp.exp(jnp.where(mk["incl"], _bcast_cols(ccols, CH) - crow, -jnp.inf))

    for g, (d, b, c) in enumerate(groups):
        mk, e = masks[d], st[g]
        q = rows(d, 0, b, c)
        k = rows(d, 1, b, c)
        kb = k * e["beta512"]
        kbd = jnp.concatenate([k.astype(BF16)] * GDN_HEADS, axis=0) * kbd_mask
        kq = lax.dot_general(jnp.concatenate([kb, q], axis=0).astype(BF16), kbd, NT_DIMS,
                             preferred_element_type=F32)
        neg_l = jnp.where(mk["strict"], -(kq[:CH] * e["decay"]), 0.0)
        e["intra"] = (kq[CH:] * e["decay"]).astype(BF16)
        e["p"] = eye + neg_l
        e["lm"] = neg_l

    for level in range(6):
        for e in st:
            w_bd = jnp.concatenate([e["lm"].astype(BF16)] * GDN_HEADS, axis=0) * bd_mask
            if level == 0:
                e["lm"] = _dot(e["lm"].astype(BF16), w_bd)
            elif level < 5:
                y = _dot(jnp.concatenate([e["p"], e["lm"]], axis=0).astype(BF16), w_bd)
                e["p"] = e["p"] + y[:CH]
                e["lm"] = y[CH:]
            else:
                e["p"] = e["p"] + _dot(e["p"].astype(BF16), w_bd)

    for g, (d, b, c) in enumerate(groups):
        e = st[g]
        k = rows(d, 1, b, c)
        v = rows(d, 2, b, c)
        tinv = e["p"].astype(BF16)
        vb = (v * e["beta512"]).astype(BF16)
        kbe = (k * e["beta512"] * jnp.exp(e["ccol512"])).astype(BF16)
        e["uw"] = []
        for pr in pairs:
            h0, h1 = 2 * pr, 2 * pr + 1
            rhs = lambda h: jnp.concatenate([vb[:, h * GDN_D:(h + 1) * GDN_D], kbe[:, h * GDN_D:(h + 1) * GDN_D]], axis=1)
            e["uw"].append(_dot(tinv[:, pr * GDN_D:(pr + 1) * GDN_D], _pair_diag(rhs(h0), rhs(h1))))

    for r in range(ncs):
        _gdn_state_stage(st, groups, r * per_rank, (r + 1) * per_rank, rows, out_refs, masks, s_ref)


def _gdn_state_stage(st, groups, lo, hi, rows, out_refs, masks, s_ref):
    pairs = range(GDN_HEADS // 2)
    for g in range(lo, hi):
        d, b, c = groups[g]
        e = st[g]
        q = rows(d, 0, b, c)
        qe = q * jnp.exp(e["ccol512"])
        e["ws"] = []
        for pr in pairs:
            lhs = []
            for h in (2 * pr, 2 * pr + 1):
                w = e["uw"][pr][:, (2 * (h % 2) + 1) * GDN_D:(2 * (h % 2) + 2) * GDN_D]
                lhs.append(jnp.concatenate([w, qe[:, h * GDN_D:(h + 1) * GDN_D]], axis=0))
            sidx = (b * 2 + d) * GDN_HEADS + 2 * pr
            sbd = _pair_diag(s_ref[sidx].astype(BF16), s_ref[sidx + 1].astype(BF16))
            e["ws"].append(_dot(jnp.concatenate(lhs, axis=1).astype(BF16), sbd))

    for g in range(lo, hi):
        d, b, c = groups[g]
        e = st[g]
        k = rows(d, 1, b, c)
        last = masks[d]["last"]
        for pr in pairs:
            vnew = []
            for h in (2 * pr, 2 * pr + 1):
                u = e["uw"][pr][:, 2 * (h % 2) * GDN_D:(2 * (h % 2) + 1) * GDN_D]
                vnew.append((u - e["ws"][pr][:CH, (h % 2) * GDN_D:(h % 2 + 1) * GDN_D]).astype(BF16))
            glast = [e["ccols"][h][last:last + 1, :] for h in (2 * pr, 2 * pr + 1)]
            kd = [k[:, h * GDN_D:(h + 1) * GDN_D] * jnp.exp(gl - e["ccol512"][:, h * GDN_D:(h + 1) * GDN_D])
                  for h, gl in zip((2 * pr, 2 * pr + 1), glast)]
            kdt = jnp.concatenate(kd, axis=0).T.astype(BF16)
            lhs = jnp.concatenate([e["intra"][:, pr * GDN_D:(pr + 1) * GDN_D], kdt], axis=0)
            prod = _dot(lhs, _pair_diag(vnew[0], vnew[1]))
            out_refs[d][b, c * CH:(c + 1) * CH, 2 * pr * GDN_D:(2 * pr + 2) * GDN_D] = (
                e["ws"][pr][CH:, :] + prod[:CH]).astype(BF16)
            for i, h in enumerate((2 * pr, 2 * pr + 1)):
                sidx = (b * 2 + d) * GDN_HEADS + h
                s_ref[sidx] = s_ref[sidx] * jnp.exp(glast[i]) + prod[CH:, i * GDN_D:(i + 1) * GDN_D]


def _gdn_scan(q, k, v, bg, gt, *, b, s):
    ncs = SCAN_CHUNKS if (s // CH) % SCAN_CHUNKS == 0 else 1
    n = s // (CH * ncs)
    fwd = lambda i: (0, i, 0)
    bwd = lambda i: (0, n - 1 - i, 0)
    fwd_t = lambda i: (0, i, 0, 0)
    bwd_t = lambda i: (0, n - 1 - i, 0, 0)
    wide = lambda im: pl.BlockSpec((b, ncs * CH, GDN_W), im)
    specs = lambda im, imt: [wide(im), wide(im), wide(im), pl.BlockSpec((b, ncs * CH, N_BA), im),
                             pl.BlockSpec((b, ncs, 2 * GDN_HEADS, CH), imt)]
    out = jax.ShapeDtypeStruct((b, s, GDN_W), BF16)
    return pl.pallas_call(
        _gdn_scan_body,
        out_shape=(out, out),
        grid=(n,),
        in_specs=specs(fwd, fwd_t) + specs(bwd, bwd_t),
        out_specs=(wide(fwd), wide(bwd)),
        scratch_shapes=[pltpu.VMEM((b * 2 * GDN_HEADS, GDN_D, GDN_D), F32)],
        compiler_params=_grid_params(1),
        name="gdn_scan",
    )(q, k, v, bg, gt, q, k, v, bg, gt)


def _merge_body(x_ref, of_ref, ob_ref, sz_ref, sg_ref, ot_ref, gnw_ref, wgp_ref, wmp_ref, wo_ref, o_ref):
    o = of_ref[...].astype(F32) + ob_ref[...].astype(F32)
    gnw = gnw_ref[...]
    heads = []
    for h in range(GDN_HEADS):
        oh = o[:, h * GDN_D:(h + 1) * GDN_D]
        heads.append(_rms(oh, gnw))
    on = jnp.concatenate(heads, axis=1) * sz_ref[...]
    ya = _dot(on.astype(BF16), wgp_ref[...])
    acc_t = ot_ref[0, :, 0].astype(F32)
    ot = (acc_t[:, :MLA_V, :] / acc_t[:, MLA_V:MLA_V + 1, :]).reshape(MLA_HEADS * MLA_V, -1)
    yb = _dot(ot.T.astype(BF16), wmp_ref[...])
    d = ya.shape[1]
    y = sg_ref[:, :d] * ya + sg_ref[:, d:] * yb
    o_ref[...] = x_ref[...] + _dot(y.astype(BF16), wo_ref[...])


def _merge(x, of, ob, sz, sg, ot, gnw, wgp, wmp, wo, *, b, s, tm):
    t, d = x.shape
    nsteps = s // tm
    row = lambda n: pl.BlockSpec((tm, n), lambda bi, i: (bi * nsteps + i, 0))
    return pl.pallas_call(
        _merge_body,
        out_shape=jax.ShapeDtypeStruct((t, d), F32),
        grid=(b, nsteps),
        in_specs=[row(d), row(GDN_W), row(GDN_W), row(GDN_W), row(2 * d),
                  pl.BlockSpec((1, MLA_HEADS, 1, V_ROWS, tm), lambda bi, i: (bi, 0, i, 0, 0)),
                  _const_spec((1, GDN_D)), _const_spec(wgp.shape), _const_spec(wmp.shape), _const_spec(wo.shape)],
        out_specs=row(d),
        compiler_params=_grid_params(2),
        name="merge",
    )(x, of, ob, sz, sg, ot, gnw, wgp, wmp, wo)


def _pack_w_in(w):
    d = w.shape[0]
    zeros = lambda n: jnp.zeros((d, n), w.dtype)
    o = 4 * GDN_W
    ba = w[:, o:o + N_BA]
    o += N_BA
    cq = w[:, o:o + MLA_Q_LORA]
    o += MLA_Q_LORA
    ckv = w[:, o:o + MLA_KV_LORA]
    o += MLA_KV_LORA
    kr = w[:, o:o + MLA_ROPE]
    o += MLA_ROPE
    gates = w[:, o:]
    half = MLA_ROPE // 2
    tail = zeros(HEAD_LANES - MLA_NOPE - MLA_ROPE)
    kr_main = jnp.concatenate([zeros(MLA_NOPE), kr, tail], axis=1)
    kr_swap = jnp.concatenate([zeros(MLA_NOPE), kr[:, half:], kr[:, :half], tail], axis=1)
    packed = jnp.concatenate([w[:, :4 * GDN_W], ba, zeros(128 - N_BA), cq, ckv, kr_main, kr_swap, gates], axis=1)
    return packed.astype(BF16), ba.T.astype(BF16)


def _pack_w_uq(w):
    r = w.shape[0]
    qk = MLA_NOPE + MLA_ROPE
    half = MLA_ROPE // 2
    tail = jnp.zeros((r, HEAD_LANES - qk), w.dtype)
    znope = jnp.zeros((r, MLA_NOPE), w.dtype)
    main, swap = [], []
    for h in range(MLA_HEADS):
        nope = w[:, h * qk:h * qk + MLA_NOPE]
        rope = w[:, h * qk + MLA_NOPE:(h + 1) * qk]
        main += [nope, rope, tail]
        swap += [znope, rope[:, half:], rope[:, :half], tail]
    return jnp.concatenate(main, axis=1).T.astype(BF16), jnp.concatenate(swap, axis=1).T.astype(BF16)


def _pack_w_ukv(w):
    r = w.shape[0]
    hw = MLA_NOPE + MLA_V
    ks, vs = [], []
    for h in range(MLA_HEADS):
        ks += [w[:, h * hw:h * hw + MLA_NOPE], jnp.zeros((r, HEAD_LANES - MLA_NOPE), w.dtype)]
        vs += [w[:, h * hw + MLA_NOPE:(h + 1) * hw], jnp.zeros((r, V_ROWS - MLA_V), w.dtype)]
    return jnp.concatenate(ks, axis=1).astype(BF16), jnp.concatenate(vs, axis=1).T.astype(BF16)


def _ones_rows():
    idx = jnp.arange(MLA_HEADS * V_ROWS) % V_ROWS
    return (idx == MLA_V).astype(F32)[:, None]


def _lane_pad(v, lo, width):
    return jnp.zeros((1, width), v.dtype).at[0, lo:lo + v.shape[0]].set(v)


def kernel(x, positions, norm_ffn1, ffn1_w_gate, ffn1_w_up, ffn1_w_down, norm_mix, w_in, gdn_conv, gdn_A_log,
           gdn_dt_bias, gdn_norm, gdn_proj, mla_q_norm, mla_w_uq, mla_kv_norm, mla_w_ukv, mla_proj, w_out,
           norm_ffn2, ffn2_w_gate, ffn2_w_up, ffn2_w_down, final_norm):
    b, s, d = x.shape
    t = b * s
    depth = w_in.shape[0]
    tm = 512 if s % 2048 == 0 else 256
    ffn_tm = 512
    tq = tk = tm

    cos, sin, cost, sint = _rope_tables(positions, tm)
    ones = _ones_rows()
    xf = x.reshape(t, d)
    row = lambda v: v.reshape(1, -1)
    for l in range(depth):
        xf = _ffn(xf, row(norm_ffn1[l]), ffn1_w_gate[l].astype(BF16), ffn1_w_up[l].astype(BF16),
                  ffn1_w_down[l].astype(BF16), row(final_norm), final_norm=False, tm=ffn_tm)

        w_packed, w_bat = _pack_w_in(w_in[l])
        alog = gdn_A_log[l].reshape(-1)
        dtb = gdn_dt_bias[l].reshape(-1)
        qkv, sz, bg, gt, cqn, ckvn, kr, sg = _inproj(
            xf, row(norm_mix[l]), w_packed, w_bat, _lane_pad(alog, 2 * GDN_HEADS, 128),
            _lane_pad(dtb, 2 * GDN_HEADS, 128), alog[:, None], dtb[:, None], row(mla_q_norm[l]),
            row(mla_kv_norm[l]), cos, sin, tm=tm)

        qn, kn, vv = _gdn_prep(qkv, gdn_conv[l], s=s, tm=tm)
        seq = lambda a: a.reshape(b, s, a.shape[-1])
        gt_chunks = gt.reshape(2 * GDN_HEADS, b, s // CH, CH).transpose(1, 2, 0, 3)
        of, ob = _gdn_scan(seq(qn), seq(kn), seq(vv), seq(bg), gt_chunks, b=b, s=s)
        of, ob = of.reshape(t, GDN_W), ob.reshape(t, GDN_W)

        wqm, wqp = _pack_w_uq(mla_w_uq[l])
        wk, wvt = _pack_w_ukv(mla_w_ukv[l])
        qt, kk, vt = _mla_prep(cqn, ckvn, kr, cost, sint, wqm, wqp, wk, wvt, ones, b=b, s=s, tk=tk)
        ot = _attention(qt, kk, vt, b=b, s=s, tq=tq, tk=tk)

        xf = _merge(xf, of, ob, sz, sg, ot, row(gdn_norm[l]), gdn_proj[l].astype(BF16), mla_proj[l].astype(BF16),
                    w_out[l].astype(BF16), b=b, s=s, tm=tm)

        xf = _ffn(xf, row(norm_ffn2[l]), ffn2_w_gate[l].astype(BF16), ffn2_w_up[l].astype(BF16),
                  ffn2_w_down[l].astype(BF16), row(final_norm), final_norm=(l == depth - 1), tm=ffn_tm)
    return xf.reshape(b, s, d)
```

```python
import functools

import jax
import jax.numpy as jnp
from jax import lax
from jax.experimental import pallas as pl
from jax.experimental.pallas import tpu as pltpu

F32 = jnp.float32
BF16 = jnp.bfloat16

EPS = 1e-6
RES_HALF = 0.5
GDN_HEADS = 4
GDN_D = 128
GDN_CONV = 5
GDN_CHUNK = 64
MLA_HEADS = 8
MLA_NOPE = 64
MLA_ROPE = 32
MLA_V = 64
MLA_Q_LORA = 384
MLA_KV_LORA = 256
ROPE_THETA = 10000.0
HEAD_LANES = 128
V_ROWS = 80
LOG2E = 1.4426950408889634
NEG_BIG = -1e30

VMEM_LIMIT_BYTES = 56 * 1024 * 1024

NT_DIMS = (((1,), (1,)), ((), ()))


def _grid_params(n, flags=None):
    return pltpu.CompilerParams(dimension_semantics=("arbitrary",) * n, vmem_limit_bytes=VMEM_LIMIT_BYTES,
                                flags=flags)


def _const_spec(shape):
    nd = len(shape)
    return pl.BlockSpec(shape, lambda *_: (0,) * nd, pipeline_mode=pl.Buffered(1))


def _rms(x, w):
    return x * lax.rsqrt(jnp.mean(x * x, axis=-1, keepdims=True) + EPS) * w


def _silu(x):
    return x * jax.nn.sigmoid(x)


def _softplus(x):
    return jnp.maximum(x, 0.0) + jnp.log1p(jnp.exp(-jnp.abs(x)))


def _dot(a, b):
    return jnp.dot(a, b, preferred_element_type=F32)


def _ffn_body(x_ref, nw_ref, wg_ref, wu_ref, wd_ref, fw_ref, o_ref, *, final_norm):
    x = x_ref[...]
    hb = _rms(x, nw_ref[...]).astype(BF16)
    g = _dot(hb, wg_ref[...])
    u = _dot(hb, wu_ref[...])
    a = (_silu(g) * u).astype(BF16)
    y = x + RES_HALF * _dot(a, wd_ref[...])
    if final_norm:
        y = _rms(y, fw_ref[...])
    o_ref[...] = y


def _ffn(x, nw, wg, wu, wd, fw, *, final_norm, tm):
    t, d = x.shape
    ff = wg.shape[1]
    row = pl.BlockSpec((tm, d), lambda i: (i, 0))
    return pl.pallas_call(
        functools.partial(_ffn_body, final_norm=final_norm),
        out_shape=jax.ShapeDtypeStruct((t, d), F32),
        grid=(t // tm,),
        in_specs=[row, _const_spec((1, d)), _const_spec((d, ff)), _const_spec((d, ff)),
                  _const_spec((ff, d)), _const_spec((1, d))],
        out_specs=row,
        compiler_params=_grid_params(1),
        name="ffn",
    )(x, nw, wg, wu, wd, fw)


def _rope_body(post_ref, fcol_ref, cos_ref, sin_ref, cost_ref, sint_ref):
    ang = fcol_ref[...] * post_ref[0].astype(F32)
    c = jnp.cos(ang)
    s = jnp.sin(ang)
    tm = ang.shape[1]
    pad = HEAD_LANES - MLA_NOPE - MLA_ROPE
    cost = jnp.concatenate([jnp.ones((MLA_NOPE, tm), F32), c, c, jnp.ones((pad, tm), F32)], axis=0)
    sint = jnp.concatenate([jnp.zeros((MLA_NOPE, tm), F32), -s, s, jnp.zeros((pad, tm), F32)], axis=0)
    cost_ref[0] = cost
    sint_ref[0] = sint
    cos_ref[...] = cost.T
    sin_ref[...] = sint.T


def _rope_tables(positions, tm):
    b, s = positions.shape
    t = b * s
    inv_freq = jnp.power(ROPE_THETA, -jnp.arange(0, MLA_ROPE, 2, dtype=F32) / MLA_ROPE)
    nsteps = s // tm
    return pl.pallas_call(
        _rope_body,
        out_shape=(jax.ShapeDtypeStruct((t, HEAD_LANES), F32), jax.ShapeDtypeStruct((t, HEAD_LANES), F32),
                   jax.ShapeDtypeStruct((b, HEAD_LANES, s), F32), jax.ShapeDtypeStruct((b, HEAD_LANES, s), F32)),
        grid=(b, nsteps),
        in_specs=[pl.BlockSpec((1, 1, tm), lambda bi, i: (bi, 0, i)), _const_spec((MLA_ROPE // 2, 1))],
        out_specs=(pl.BlockSpec((tm, HEAD_LANES), lambda bi, i: (bi * nsteps + i, 0)),
                   pl.BlockSpec((tm, HEAD_LANES), lambda bi, i: (bi * nsteps + i, 0)),
                   pl.BlockSpec((1, HEAD_LANES, tm), lambda bi, i: (bi, 0, i)),
                   pl.BlockSpec((1, HEAD_LANES, tm), lambda bi, i: (bi, 0, i))),
        compiler_params=_grid_params(2),
        name="rope_tables",
    )(positions.reshape(b, 1, s), inv_freq[:, None])


GDN_W = GDN_HEADS * GDN_D
SEG_QKV = (0, 3 * GDN_W)
SEG_Z = (SEG_QKV[1], SEG_QKV[1] + GDN_W)
SEG_BA = (SEG_Z[1], SEG_Z[1] + 128)
SEG_CQ = (SEG_BA[1], SEG_BA[1] + MLA_Q_LORA)
SEG_CKV = (SEG_CQ[1], SEG_CQ[1] + MLA_KV_LORA)
SEG_KRM = (SEG_CKV[1], SEG_CKV[1] + HEAD_LANES)
SEG_KRP = (SEG_KRM[1], SEG_KRM[1] + HEAD_LANES)
SEG_GATE = (SEG_KRP[1], SEG_KRP[1] + 2048)
N_BA = 4 * GDN_HEADS


def _inproj_body(x_ref, nw_ref, w_ref, wbat_ref, alog_ref, dtb_ref, alogt_ref, dtbt_ref, qnw_ref, kvnw_ref,
                 cos_ref, sin_ref, qkv_ref, sz_ref, bg_ref, gt_ref, cqn_ref, ckvn_ref, kr_ref, sg_ref):
    hb = _rms(x_ref[...], nw_ref[...]).astype(BF16)

    def seg(bounds):
        return _dot(hb, w_ref[:, bounds[0]:bounds[1]])

    qkv_ref[...] = seg(SEG_QKV)
    sz_ref[...] = _silu(seg(SEG_Z)).astype(BF16)
    ba = seg(SEG_BA)
    lane = lax.broadcasted_iota(jnp.int32, ba.shape, 1)
    decay = -jnp.exp(alog_ref[...]) * _softplus(ba + dtb_ref[...])
    bg_ref[...] = jnp.where(lane < 2 * GDN_HEADS, jax.nn.sigmoid(ba), decay)[:, :N_BA]
    bat = lax.dot_general(wbat_ref[...], hb, NT_DIMS, preferred_element_type=F32)
    at = bat[2 * GDN_HEADS:, :]
    gt_ref[...] = -jnp.exp(alogt_ref[...]) * _softplus(at + dtbt_ref[...])
    cqn_ref[...] = _rms(seg(SEG_CQ), qnw_ref[...]).astype(BF16)
    ckvn_ref[...] = _rms(seg(SEG_CKV), kvnw_ref[...]).astype(BF16)
    kr_ref[...] = seg(SEG_KRM) * cos_ref[...] + seg(SEG_KRP) * sin_ref[...]
    sg_ref[...] = jax.nn.sigmoid(seg(SEG_GATE)).astype(BF16)


def _inproj(x, nw, w, wbat, alog, dtb, alogt, dtbt, qnw, kvnw, cos, sin, *, tm):
    t, d = x.shape
    row = lambda n: pl.BlockSpec((tm, n), lambda i: (i, 0))
    out_shape = (
        jax.ShapeDtypeStruct((t, 3 * GDN_W), F32),
        jax.ShapeDtypeStruct((t, GDN_W), BF16),
        jax.ShapeDtypeStruct((t, N_BA), F32),
        jax.ShapeDtypeStruct((2 * GDN_HEADS, t), F32),
        jax.ShapeDtypeStruct((t, MLA_Q_LORA), BF16),
        jax.ShapeDtypeStruct((t, MLA_KV_LORA), BF16),
        jax.ShapeDtypeStruct((t, HEAD_LANES), F32),
        jax.ShapeDtypeStruct((t, 2048), BF16),
    )
    out_specs = (row(3 * GDN_W), row(GDN_W), row(N_BA), pl.BlockSpec((2 * GDN_HEADS, tm), lambda i: (0, i)),
                 row(MLA_Q_LORA), row(MLA_KV_LORA), row(HEAD_LANES), row(2048))
    return pl.pallas_call(
        _inproj_body,
        out_shape=out_shape,
        grid=(t // tm,),
        in_specs=[row(d), _const_spec((1, d)), _const_spec(w.shape), _const_spec(wbat.shape),
                  _const_spec((1, 128)), _const_spec((1, 128)),
                  _const_spec((2 * GDN_HEADS, 1)), _const_spec((2 * GDN_HEADS, 1)),
                  _const_spec((1, MLA_Q_LORA)), _const_spec((1, MLA_KV_LORA)),
                  row(HEAD_LANES), row(HEAD_LANES)],
        out_specs=out_specs,
        compiler_params=_grid_params(1),
        name="inproj",
    )(x, nw, w, wbat, alog, dtb, alogt, dtbt, qnw, kvnw, cos, sin)


def _mla_prep_body(cqn_ref, ckvn_ref, kr_ref, cost_ref, sint_ref, wqm_ref, wqp_ref, wk_ref, wvt_ref, ones_ref,
                   qt_ref, k_ref, vt_ref):
    cqn = cqn_ref[...]
    qm = lax.dot_general(wqm_ref[...], cqn, NT_DIMS, preferred_element_type=F32)
    qp = lax.dot_general(wqp_ref[...], cqn, NT_DIMS, preferred_element_type=F32)
    qscale = (MLA_NOPE + MLA_ROPE) ** -0.5 * LOG2E
    cost = cost_ref[0] * qscale
    sint = sint_ref[0] * qscale
    ckvn = ckvn_ref[...]
    km = _dot(ckvn, wk_ref[...])
    kr = kr_ref[...]
    for h in range(MLA_HEADS):
        grp = slice(h * HEAD_LANES, (h + 1) * HEAD_LANES)
        qt_ref[0, h, 0] = (qm[grp, :] * cost + qp[grp, :] * sint).astype(BF16)
        k_ref[:, grp] = (km[:, grp] + kr).astype(BF16)
    vt = lax.dot_general(wvt_ref[...], ckvn, NT_DIMS, preferred_element_type=F32)
    vt_ref[0, 0] = (vt + ones_ref[...]).astype(BF16)


def _mla_prep(cqn, ckvn, kr, cost, sint, wqm, wqp, wk, wvt, ones, *, b, s, tq, tk):
    t = b * s
    nsteps = s // tq
    per_key_block = tk // tq
    hl = MLA_HEADS * HEAD_LANES
    vr = MLA_HEADS * V_ROWS
    row = lambda n: pl.BlockSpec((tq, n), lambda bi, i: (bi * nsteps + i, 0))
    tr = pl.BlockSpec((1, HEAD_LANES, tq), lambda bi, i: (bi, 0, i))
    return pl.pallas_call(
        _mla_prep_body,
        out_shape=(jax.ShapeDtypeStruct((b, MLA_HEADS, nsteps, HEAD_LANES, tq), BF16),
                   jax.ShapeDtypeStruct((t, hl), BF16),
                   jax.ShapeDtypeStruct((b, s // tk, vr, tk), BF16)),
        grid=(b, nsteps),
        in_specs=[row(MLA_Q_LORA), row(MLA_KV_LORA), row(HEAD_LANES), tr, tr,
                  _const_spec(wqm.shape), _const_spec(wqp.shape), _const_spec(wk.shape), _const_spec(wvt.shape),
                  _const_spec((vr, 1))],
        out_specs=(pl.BlockSpec((1, MLA_HEADS, 1, HEAD_LANES, tq), lambda bi, i: (bi, 0, i, 0, 0)), row(hl),
                   pl.BlockSpec((1, 1, vr, tq), lambda bi, i: (bi, i // per_key_block, 0, i % per_key_block))),
        compiler_params=_grid_params(2),
        name="mla_prep",
    )(cqn, ckvn, kr, cost, sint, wqm, wqp, wk, wvt, ones)


ATTN_GENS = 3
KEY_BLOCKS_PER_TILE = 2
ATTN_UNROLL = 4
ATTN_LAG = 2
KEY_SLICE = 128
PV_DEPTH = 256


def _attn_body(qt_ref, k_ref, vt_ref, ot_ref, s_buf, p_buf, bm_buf, a_buf, m_ref, acc_ref, *, nblk):
    nq = qt_ref.shape[2]
    tk = p_buf.shape[1]
    nblocks = nq * nblk
    nslices = tk // KEY_SLICE
    pv_every = PV_DEPTH // KEY_SLICE
    m_ref[...] = jnp.full(m_ref.shape, NEG_BIG, F32)
    acc_ref[...] = jnp.zeros(acc_ref.shape, F32)

    def step(t, phase, do_scores=True, do_softmax=True, do_pv=True):
        g_s, g_m, g_p = phase, (phase + 1) % ATTN_GENS, (phase + 2) % ATTN_GENS
        n_m, n_p = t - ATTN_LAG, t - 2 * ATTN_LAG
        if do_scores:
            j_s = t % nblk
            start = j_s * tk if isinstance(j_s, int) else pl.multiple_of(j_s * tk, tk)
            qt = qt_ref[0, 0, t // nblk]
        if do_softmax:
            m_old = jnp.where(n_m % nblk == 0, NEG_BIG, m_ref[...])
            m_new = jnp.maximum(m_old, bm_buf[g_m])
            a_buf[g_m] = jnp.exp2(m_old - m_new)
            m_ref[...] = m_new
        if do_pv:
            acc = acc_ref[...] * a_buf[g_p]
            vt = vt_ref.at[0, n_p % nblk]
        for c in range(nslices):
            rows = slice(c * KEY_SLICE, (c + 1) * KEY_SLICE)
            if do_pv and c % pv_every == pv_every - 1:
                deep = slice((c + 1 - pv_every) * KEY_SLICE, (c + 1) * KEY_SLICE)
                acc = acc + _dot(vt[:, deep], p_buf[g_p, deep, :])
            if do_softmax:
                p_buf[g_m, rows, :] = jnp.exp2(s_buf[g_m, rows, :] - m_new).astype(BF16)
            if do_scores and c == 0:
                s = _dot(k_ref[pl.ds(start, tk), :], qt)
                s_buf[g_s] = s
                bm_buf[g_s] = jnp.max(s, axis=0, keepdims=True)
        if do_pv:
            acc_ref[...] = acc
            return n_p // nblk, acc
        return None

    def write(out):
        if out is not None:
            ot_ref[0, 0, out[0]] = out[1]

    def static_step(t):
        write(step(t, t % ATTN_GENS, do_scores=t < nblocks, do_softmax=ATTN_LAG <= t < nblocks + ATTN_LAG,
                   do_pv=t >= 2 * ATTN_LAG))

    fill = 2 * ATTN_LAG
    for t in range(fill):
        static_step(t)
    per_iter = ATTN_GENS * ATTN_UNROLL
    nloop = max(nblocks - fill, 0) // per_iter

    def body(u, carry):
        t0 = fill + per_iter * u
        outs = [step(t0 + i, (fill + i) % ATTN_GENS) for i in range(per_iter)]
        for out in outs:
            write(out)
        return carry

    lax.fori_loop(0, nloop, body, 0)
    for t in range(fill + per_iter * nloop, nblocks + fill):
        static_step(t)


def _attention(qt, k, vt, *, b, s, tq, tk):
    nq = s // tq
    nk = s // tk
    assert tk % PV_DEPTH == 0 and PV_DEPTH % KEY_SLICE == 0
    return pl.pallas_call(
        functools.partial(_attn_body, nblk=nk),
        out_shape=jax.ShapeDtypeStruct((b, MLA_HEADS, nq, V_ROWS, tq), F32),
        grid=(b, MLA_HEADS),
        in_specs=[pl.BlockSpec((1, 1, nq, HEAD_LANES, tq), lambda bi, h: (bi, h, 0, 0, 0)),
                  pl.BlockSpec((s, HEAD_LANES), lambda bi, h: (bi, h)),
                  pl.BlockSpec((1, nk, V_ROWS, tk), lambda bi, h: (bi, 0, h, 0))],
        out_specs=pl.BlockSpec((1, 1, nq, V_ROWS, tq), lambda bi, h: (bi, h, 0, 0, 0)),
        scratch_shapes=[pltpu.VMEM((ATTN_GENS, tk, tq), F32), pltpu.VMEM((ATTN_GENS, tk, tq), BF16),
                        pltpu.VMEM((ATTN_GENS, 1, tq), F32), pltpu.VMEM((ATTN_GENS, 1, tq), F32),
                        pltpu.VMEM((1, tq), F32), pltpu.VMEM((V_ROWS, tq), F32)],
        compiler_params=_grid_params(2),
        name="attention",
    )(qt, k, vt)


HALO = 8


def _gdn_prep_body(prev_ref, cur_ref, next_ref, cw_ref, q_ref, k_ref, v_ref, buf_ref, *, tiles_per_seq):
    i = pl.program_id(0)
    tm = cur_ref.shape[0]
    first = (i % tiles_per_seq) == 0
    last = (i % tiles_per_seq) == tiles_per_seq - 1
    buf_ref[0:HALO, :] = jnp.where(first, 0.0, prev_ref[...])
    buf_ref[HALO:HALO + tm, :] = cur_ref[...]
    buf_ref[HALO + tm:2 * HALO + tm, :] = jnp.where(last, 0.0, next_ref[...])
    pad = GDN_CONV // 2
    for grp in range(3 * GDN_HEADS):
        lanes = slice(grp * GDN_D, (grp + 1) * GDN_D)
        acc = None
        for j in range(GDN_CONV):
            lo = HALO - pad + j
            term = buf_ref[lo:lo + tm, lanes] * cw_ref[j:j + 1, lanes]
            acc = term if acc is None else acc + term
        y = _silu(acc)
        if grp < 2 * GDN_HEADS:
            y = y * lax.rsqrt(jnp.sum(y * y, axis=-1, keepdims=True) + EPS)
        if grp < GDN_HEADS:
            q_ref[:, lanes] = y * GDN_D ** -0.5
        elif grp < 2 * GDN_HEADS:
            k_ref[:, slice(lanes.start - GDN_W, lanes.stop - GDN_W)] = y
        else:
            v_ref[:, slice(lanes.start - 2 * GDN_W, lanes.stop - 2 * GDN_W)] = y


def _gdn_prep(qkv, conv_w, *, s, tm):
    t, c = qkv.shape
    tiles_per_seq = s // tm
    hb = tm // HALO
    nh = t // HALO
    out = jax.ShapeDtypeStruct((t, GDN_W), F32)
    row = pl.BlockSpec((tm, GDN_W), lambda i: (i, 0))
    return pl.pallas_call(
        functools.partial(_gdn_prep_body, tiles_per_seq=tiles_per_seq),
        out_shape=(out, out, out),
        grid=(t // tm,),
        in_specs=[pl.BlockSpec((HALO, c), lambda i: (jnp.maximum(i * hb - 1, 0), 0)),
                  pl.BlockSpec((tm, c), lambda i: (i, 0)),
                  pl.BlockSpec((HALO, c), lambda i: (jnp.minimum((i + 1) * hb, nh - 1), 0)),
                  _const_spec(conv_w.shape)],
        out_specs=(row, row, row),
        scratch_shapes=[pltpu.VMEM((tm + 2 * HALO, c), F32)],
        compiler_params=_grid_params(1),
        name="gdn_prep",
    )(qkv, qkv, qkv, conv_w)


CH = GDN_CHUNK
HC = GDN_HEADS * CH
SCAN_CHUNKS = 2


def _split3(x):
    hi = x.astype(BF16)
    r = x - hi.astype(F32)
    mid = r.astype(BF16)
    lo = (r - mid.astype(F32)).astype(BF16)
    return hi, mid, lo


def _bcast_cols(cols, width):
    if width == 128:
        return jnp.concatenate([jnp.broadcast_to(c, (CH, 128)) for c in cols], axis=1)
    lane = lax.broadcasted_iota(jnp.int32, (CH, 128), 1)
    lo = lane < 64
    pair = lambda a, b: jnp.where(lo, jnp.broadcast_to(a, (CH, 128)), jnp.broadcast_to(b, (CH, 128)))
    return jnp.concatenate([pair(cols[0], cols[1]), pair(cols[2], cols[3])], axis=1)


def _dir_masks(d):
    r64 = lax.broadcasted_iota(jnp.int32, (CH, CH), 0)
    c64 = lax.broadcasted_iota(jnp.int32, (CH, CH), 1)
    r256 = lax.broadcasted_iota(jnp.int32, (CH, HC), 0)
    c256 = lax.broadcasted_iota(jnp.int32, (CH, HC), 1) & (CH - 1)
    if d == 0:
        tri, incl, strict, trit, last = (c64 <= r64), (c256 <= r256), (c256 < r256), (r256 <= c256), CH - 1
    else:
        tri, incl, strict, trit, last = (c64 >= r64), (c256 >= r256), (c256 > r256), (r256 >= c256), 0
    tri3 = jnp.concatenate([tri.astype(BF16)] * 3, axis=1)
    trit3 = jnp.concatenate([trit.astype(BF16)] * 3, axis=0)
    return dict(tri3=tri3, trit3=trit3, incl=incl, strict=strict, last=last)


def _pair_diag(a, b):
    z = jnp.zeros(a.shape, a.dtype)
    return jnp.concatenate([jnp.concatenate([a, z], axis=1), jnp.concatenate([z, b], axis=1)], axis=0)


def _gdn_scan_body(qf_ref, kf_ref, vf_ref, bgf_ref, gtf_ref, qb_ref, kb_ref, vb_ref, bgb_ref, gtb_ref,
                   of_ref, ob_ref, s_ref):
    @pl.when(pl.program_id(0) == 0)
    def _():
        s_ref[...] = jnp.zeros(s_ref.shape, F32)

    nb = qf_ref.shape[0]
    ncs = qf_ref.shape[1] // CH
    in_refs = ((qf_ref, kf_ref, vf_ref, bgf_ref, gtf_ref), (qb_ref, kb_ref, vb_ref, bgb_ref, gtb_ref))
    out_refs = (of_ref, ob_ref)
    masks = (_dir_masks(0), _dir_masks(1))
    groups = [(d, b, r if d == 0 else ncs - 1 - r) for r in range(ncs) for b in range(nb) for d in (0, 1)]
    per_rank = 2 * nb

    def rows(d, idx, b, c):
        return in_refs[d][idx][b, c * CH:(c + 1) * CH, :]
    heads = range(GDN_HEADS)
    pairs = range(GDN_HEADS // 2)

    r256 = lax.broadcasted_iota(jnp.int32, (CH, HC), 0)
    c256 = lax.broadcasted_iota(jnp.int32, (CH, HC), 1) & (CH - 1)
    eye = (c256 == r256).astype(F32)
    bd_mask = ((lax.broadcasted_iota(jnp.int32, (HC, HC), 0) >> 6)
               == (lax.broadcasted_iota(jnp.int32, (HC, HC), 1) >> 6)).astype(BF16)
    kbd_mask = ((lax.broadcasted_iota(jnp.int32, (HC, GDN_W), 0) >> 6)
                == (lax.broadcasted_iota(jnp.int32, (HC, GDN_W), 1) >> 7)).astype(BF16)
    r8 = lax.broadcasted_iota(jnp.int32, (2 * GDN_HEADS, HC), 0)
    h8 = lax.broadcasted_iota(jnp.int32, (2 * GDN_HEADS, HC), 1) >> 6

    st = []
    for d, b, c in groups:
        mk = masks[d]
        bg = rows(d, 3, b, c)
        gt = in_refs[d][4][b, c]
        cs = _dot(mk["tri3"], jnp.concatenate(_split3(bg), axis=0))
        cst = _dot(jnp.concatenate(_split3(gt), axis=1), mk["trit3"])
        st.append(dict(bg=bg, cs=cs, cst=cst))

    for g, (d, b, c) in enumerate(groups):
        mk, e = masks[d], st[g]
        crow = jnp.sum(jnp.where(r8 == d * GDN_HEADS + h8, e["cst"], 0.0), axis=0, keepdims=True)
        goff = 2 * GDN_HEADS + d * GDN_HEADS
        ccols = [e["cs"][:, goff + h:goff + h + 1] for h in heads]
        betas = [e["bg"][:, d * GDN_HEADS + h:d * GDN_HEADS + h + 1] for h in heads]
        e["ccols"] = ccols
        e["ccol512"] = _bcast_cols(ccols, GDN_D)
        e["beta512"] = _bcast_cols(betas, GDN_D)
        e["decay"] = jnp.exp(jnp.where(mk["incl"], _bcast_cols(ccols, CH) - crow, -jnp.inf))

    for g, (d, b, c) in enumerate(groups):
        mk, e = masks[d], st[g]
        q = rows(d, 0, b, c)
        k = rows(d, 1, b, c)
        kb = k * e["beta512"]
        kbd = jnp.concatenate([k.astype(BF16)] * GDN_HEADS, axis=0) * kbd_mask
        kq = lax.dot_general(jnp.concatenate([kb, q], axis=0).astype(BF16), kbd, NT_DIMS,
                             preferred_element_type=F32)
        neg_l = jnp.where(mk["strict"], -(kq[:CH] * e["decay"]), 0.0)
        e["intra"] = (kq[CH:] * e["decay"]).astype(BF16)
        e["p"] = eye + neg_l
        e["lm"] = neg_l

    for level in range(6):
        for e in st:
            w_bd = jnp.concatenate([e["lm"].astype(BF16)] * GDN_HEADS, axis=0) * bd_mask
            if level == 0:
                e["lm"] = _dot(e["lm"].astype(BF16), w_bd)
            elif level < 5:
                y = _dot(jnp.concatenate([e["p"], e["lm"]], axis=0).astype(BF16), w_bd)
                e["p"] = e["p"] + y[:CH]
                e["lm"] = y[CH:]
            else:
                e["p"] = e["p"] + _dot(e["p"].astype(BF16), w_bd)

    for g, (d, b, c) in enumerate(groups):
        e = st[g]
        k = rows(d, 1, b, c)
        v = rows(d, 2, b, c)
        tinv = e["p"].astype(BF16)
        vb = (v * e["beta512"]).astype(BF16)
        kbe = (k * e["beta512"] * jnp.exp(e["ccol512"])).astype(BF16)
        e["uw"] = []
        for pr in pairs:
            h0, h1 = 2 * pr, 2 * pr + 1
            rhs = lambda h: jnp.concatenate([vb[:, h * GDN_D:(h + 1) * GDN_D], kbe[:, h * GDN_D:(h + 1) * GDN_D]], axis=1)
            e["uw"].append(_dot(tinv[:, pr * GDN_D:(pr + 1) * GDN_D], _pair_diag(rhs(h0), rhs(h1))))

    for r in range(ncs):
        _gdn_state_stage(st, groups, r * per_rank, (r + 1) * per_rank, rows, out_refs, masks, s_ref)


def _gdn_state_stage(st, groups, lo, hi, rows, out_refs, masks, s_ref):
    pairs = range(GDN_HEADS // 2)
    for g in range(lo, hi):
        d, b, c = groups[g]
        e = st[g]
        q = rows(d, 0, b, c)
        qe = q * jnp.exp(e["ccol512"])
        e["ws"] = []
        for pr in pairs:
            lhs = []
            for h in (2 * pr, 2 * pr + 1):
                w = e["uw"][pr][:, (2 * (h % 2) + 1) * GDN_D:(2 * (h % 2) + 2) * GDN_D]
                lhs.append(jnp.concatenate([w, qe[:, h * GDN_D:(h + 1) * GDN_D]], axis=0))
            sidx = (b * 2 + d) * GDN_HEADS + 2 * pr
            sbd = _pair_diag(s_ref[sidx].astype(BF16), s_ref[sidx + 1].astype(BF16))
            e["ws"].append(_dot(jnp.concatenate(lhs, axis=1).astype(BF16), sbd))

    for g in range(lo, hi):
        d, b, c = groups[g]
        e = st[g]
        k = rows(d, 1, b, c)
        last = masks[d]["last"]
        for pr in pairs:
            vnew = []
            for h in (2 * pr, 2 * pr + 1):
                u = e["uw"][pr][:, 2 * (h % 2) * GDN_D:(2 * (h % 2) + 1) * GDN_D]
                vnew.append((u - e["ws"][pr][:CH, (h % 2) * GDN_D:(h % 2 + 1) * GDN_D]).astype(BF16))
            glast = [e["ccols"][h][last:last + 1, :] for h in (2 * pr, 2 * pr + 1)]
            kd = [k[:, h * GDN_D:(h + 1) * GDN_D] * jnp.exp(gl - e["ccol512"][:, h * GDN_D:(h + 1) * GDN_D])
                  for h, gl in zip((2 * pr, 2 * pr + 1), glast)]
            kdt = jnp.concatenate(kd, axis=0).T.astype(BF16)
            lhs = jnp.concatenate([e["intra"][:, pr * GDN_D:(pr + 1) * GDN_D], kdt], axis=0)
            prod = _dot(lhs, _pair_diag(vnew[0], vnew[1]))
            out_refs[d][b, c * CH:(c + 1) * CH, 2 * pr * GDN_D:(2 * pr + 2) * GDN_D] = e["ws"][pr][CH:, :] + prod[:CH]
            for i, h in enumerate((2 * pr, 2 * pr + 1)):
                sidx = (b * 2 + d) * GDN_HEADS + h
                s_ref[sidx] = s_ref[sidx] * jnp.exp(glast[i]) + prod[CH:, i * GDN_D:(i + 1) * GDN_D]


def _gdn_scan(q, k, v, bg, gt, *, b, s):
    ncs = SCAN_CHUNKS if (s // CH) % SCAN_CHUNKS == 0 else 1
    n = s // (CH * ncs)
    fwd = lambda i: (0, i, 0)
    bwd = lambda i: (0, n - 1 - i, 0)
    fwd_t = lambda i: (0, i, 0, 0)
    bwd_t = lambda i: (0, n - 1 - i, 0, 0)
    wide = lambda im: pl.BlockSpec((b, ncs * CH, GDN_W), im)
    specs = lambda im, imt: [wide(im), wide(im), wide(im), pl.BlockSpec((b, ncs * CH, N_BA), im),
                             pl.BlockSpec((b, ncs, 2 * GDN_HEADS, CH), imt)]
    out = jax.ShapeDtypeStruct((b, s, GDN_W), F32)
    return pl.pallas_call(
        _gdn_scan_body,
        out_shape=(out, out),
        grid=(n,),
        in_specs=specs(fwd, fwd_t) + specs(bwd, bwd_t),
        out_specs=(wide(fwd), wide(bwd)),
        scratch_shapes=[pltpu.VMEM((b * 2 * GDN_HEADS, GDN_D, GDN_D), F32)],
        compiler_params=_grid_params(1),
        name="gdn_scan",
    )(q, k, v, bg, gt, q, k, v, bg, gt)


def _merge_body(x_ref, of_ref, ob_ref, sz_ref, sg_ref, ot_ref, gnw_ref, wgp_ref, wmp_ref, wo_ref, o_ref):
    o = of_ref[...] + ob_ref[...]
    gnw = gnw_ref[...]
    heads = []
    for h in range(GDN_HEADS):
        oh = o[:, h * GDN_D:(h + 1) * GDN_D]
        heads.append(_rms(oh, gnw))
    on = jnp.concatenate(heads, axis=1) * sz_ref[...]
    ya = _dot(on.astype(BF16), wgp_ref[...])
    acc_t = ot_ref[0, :, 0]
    ot = (acc_t[:, :MLA_V, :] / acc_t[:, MLA_V:MLA_V + 1, :]).reshape(MLA_HEADS * MLA_V, -1)
    yb = _dot(ot.T.astype(BF16), wmp_ref[...])
    d = ya.shape[1]
    y = sg_ref[:, :d] * ya + sg_ref[:, d:] * yb
    o_ref[...] = x_ref[...] + _dot(y.astype(BF16), wo_ref[...])


def _merge(x, of, ob, sz, sg, ot, gnw, wgp, wmp, wo, *, b, s, tm):
    t, d = x.shape
    nsteps = s // tm
    row = lambda n: pl.BlockSpec((tm, n), lambda bi, i: (bi * nsteps + i, 0))
    return pl.pallas_call(
        _merge_body,
        out_shape=jax.ShapeDtypeStruct((t, d), F32),
        grid=(b, nsteps),
        in_specs=[row(d), row(GDN_W), row(GDN_W), row(GDN_W), row(2 * d),
                  pl.BlockSpec((1, MLA_HEADS, 1, V_ROWS, tm), lambda bi, i: (bi, 0, i, 0, 0)),
                  _const_spec((1, GDN_D)), _const_spec(wgp.shape), _const_spec(wmp.shape), _const_spec(wo.shape)],
        out_specs=row(d),
        compiler_params=_grid_params(2),
        name="merge",
    )(x, of, ob, sz, sg, ot, gnw, wgp, wmp, wo)


def _pack_w_in(w):
    d = w.shape[0]
    zeros = lambda n: jnp.zeros((d, n), w.dtype)
    o = 4 * GDN_W
    ba = w[:, o:o + N_BA]
    o += N_BA
    cq = w[:, o:o + MLA_Q_LORA]
    o += MLA_Q_LORA
    ckv = w[:, o:o + MLA_KV_LORA]
    o += MLA_KV_LORA
    kr = w[:, o:o + MLA_ROPE]
    o += MLA_ROPE
    gates = w[:, o:]
    half = MLA_ROPE // 2
    tail = zeros(HEAD_LANES - MLA_NOPE - MLA_ROPE)
    kr_main = jnp.concatenate([zeros(MLA_NOPE), kr, tail], axis=1)
    kr_swap = jnp.concatenate([zeros(MLA_NOPE), kr[:, half:], kr[:, :half], tail], axis=1)
    packed = jnp.concatenate([w[:, :4 * GDN_W], ba, zeros(128 - N_BA), cq, ckv, kr_main, kr_swap, gates], axis=1)
    return packed.astype(BF16), ba.T.astype(BF16)


def _pack_w_uq(w):
    r = w.shape[0]
    qk = MLA_NOPE + MLA_ROPE
    half = MLA_ROPE // 2
    tail = jnp.zeros((r, HEAD_LANES - qk), w.dtype)
    znope = jnp.zeros((r, MLA_NOPE), w.dtype)
    main, swap = [], []
    for h in range(MLA_HEADS):
        nope = w[:, h * qk:h * qk + MLA_NOPE]
        rope = w[:, h * qk + MLA_NOPE:(h + 1) * qk]
        main += [nope, rope, tail]
        swap += [znope, rope[:, half:], rope[:, :half], tail]
    return jnp.concatenate(main, axis=1).T.astype(BF16), jnp.concatenate(swap, axis=1).T.astype(BF16)


def _pack_w_ukv(w):
    r = w.shape[0]
    hw = MLA_NOPE + MLA_V
    ks, vs = [], []
    for h in range(MLA_HEADS):
        ks += [w[:, h * hw:h * hw + MLA_NOPE], jnp.zeros((r, HEAD_LANES - MLA_NOPE), w.dtype)]
        vs += [w[:, h * hw + MLA_NOPE:(h + 1) * hw], jnp.zeros((r, V_ROWS - MLA_V), w.dtype)]
    return jnp.concatenate(ks, axis=1).astype(BF16), jnp.concatenate(vs, axis=1).T.astype(BF16)


def _ones_rows():
    idx = jnp.arange(MLA_HEADS * V_ROWS) % V_ROWS
    return (idx == MLA_V).astype(F32)[:, None]


def _lane_pad(v, lo, width):
    return jnp.zeros((1, width), v.dtype).at[0, lo:lo + v.shape[0]].set(v)


def kernel(x, positions, norm_ffn1, ffn1_w_gate, ffn1_w_up, ffn1_w_down, norm_mix, w_in, gdn_conv, gdn_A_log,
           gdn_dt_bias, gdn_norm, gdn_proj, mla_q_norm, mla_w_uq, mla_kv_norm, mla_w_ukv, mla_proj, w_out,
           norm_ffn2, ffn2_w_gate, ffn2_w_up, ffn2_w_down, final_norm):
    b, s, d = x.shape
    t = b * s
    depth = w_in.shape[0]
    tm = 512 if s % 2048 == 0 else 256
    ffn_tm = 512
    tq = tm
    tk = KEY_BLOCKS_PER_TILE * tm

    cos, sin, cost, sint = _rope_tables(positions, tm)
    ones = _ones_rows()
    xf = x.reshape(t, d)
    row = lambda v: v.reshape(1, -1)
    for l in range(depth):
        xf = _ffn(xf, row(norm_ffn1[l]), ffn1_w_gate[l].astype(BF16), ffn1_w_up[l].astype(BF16),
                  ffn1_w_down[l].astype(BF16), row(final_norm), final_norm=False, tm=ffn_tm)

        w_packed, w_bat = _pack_w_in(w_in[l])
        alog = gdn_A_log[l].reshape(-1)
        dtb = gdn_dt_bias[l].reshape(-1)
        qkv, sz, bg, gt, cqn, ckvn, kr, sg = _inproj(
            xf, row(norm_mix[l]), w_packed, w_bat, _lane_pad(alog, 2 * GDN_HEADS, 128),
            _lane_pad(dtb, 2 * GDN_HEADS, 128), alog[:, None], dtb[:, None], row(mla_q_norm[l]),
            row(mla_kv_norm[l]), cos, sin, tm=tm)

        qn, kn, vv = _gdn_prep(qkv, gdn_conv[l], s=s, tm=tm)
        seq = lambda a: a.reshape(b, s, a.shape[-1])
        gt_chunks = gt.reshape(2 * GDN_HEADS, b, s // CH, CH).transpose(1, 2, 0, 3)
        of, ob = _gdn_scan(seq(qn), seq(kn), seq(vv), seq(bg), gt_chunks, b=b, s=s)
        of, ob = of.reshape(t, GDN_W), ob.reshape(t, GDN_W)

        wqm, wqp = _pack_w_uq(mla_w_uq[l])
        wk, wvt = _pack_w_ukv(mla_w_ukv[l])
        qt, kk, vt = _mla_prep(cqn, ckvn, kr, cost, sint, wqm, wqp, wk, wvt, ones, b=b, s=s, tq=tq, tk=tk)
        ot = _attention(qt, kk, vt, b=b, s=s, tq=tq, tk=tk)

        xf = _merge(xf, of, ob, sz, sg, ot, row(gdn_norm[l]), gdn_proj[l].astype(BF16), mla_proj[l].astype(BF16),
                    w_out[l].astype(BF16), b=b, s=s, tm=tm)

        xf = _ffn(xf, row(norm_ffn2[l]), ffn2_w_gate[l].astype(BF16), ffn2_w_up[l].astype(BF16),
                  ffn2_w_down[l].astype(BF16), row(final_norm), final_norm=(l == depth - 1), tm=ffn_tm)
    return xf.reshape(b, s, d)
```

```python
import functools

import jax
import jax.numpy as jnp
from jax import lax
from jax.experimental import pallas as pl
from jax.experimental.pallas import tpu as pltpu

F32 = jnp.float32
BF16 = jnp.bfloat16

EPS = 1e-6
RES_HALF = 0.5
GDN_HEADS = 4
GDN_D = 128
GDN_CONV = 5
GDN_CHUNK = 64
MLA_HEADS = 8
MLA_NOPE = 64
MLA_ROPE = 32
MLA_V = 64
MLA_Q_LORA = 384
MLA_KV_LORA = 256
ROPE_THETA = 10000.0
HEAD_LANES = 128
V_ROWS = 80
LOG2E = 1.4426950408889634
NEG_BIG = -1e30

VMEM_LIMIT_BYTES = 56 * 1024 * 1024

NT_DIMS = (((1,), (1,)), ((), ()))


def _grid_params(n, flags=None):
    return pltpu.CompilerParams(dimension_semantics=("arbitrary",) * n, vmem_limit_bytes=VMEM_LIMIT_BYTES,
                                flags=flags)


def _const_spec(shape):
    nd = len(shape)
    return pl.BlockSpec(shape, lambda *_: (0,) * nd, pipeline_mode=pl.Buffered(1))


def _rms(x, w):
    return x * lax.rsqrt(jnp.mean(x * x, axis=-1, keepdims=True) + EPS) * w


def _silu(x):
    return x * jax.nn.sigmoid(x)


def _softplus(x):
    return jnp.maximum(x, 0.0) + jnp.log1p(jnp.exp(-jnp.abs(x)))


def _dot(a, b):
    return jnp.dot(a, b, preferred_element_type=F32)


def _ffn_body(x_ref, nw_ref, wg_ref, wu_ref, wd_ref, fw_ref, o_ref, *, final_norm):
    x = x_ref[...]
    hb = _rms(x, nw_ref[...]).astype(BF16)
    g = _dot(hb, wg_ref[...])
    u = _dot(hb, wu_ref[...])
    a = (_silu(g) * u).astype(BF16)
    y = x + RES_HALF * _dot(a, wd_ref[...])
    if final_norm:
        y = _rms(y, fw_ref[...])
    o_ref[...] = y


def _ffn(x, nw, wg, wu, wd, fw, *, final_norm, tm):
    t, d = x.shape
    ff = wg.shape[1]
    row = pl.BlockSpec((tm, d), lambda i: (i, 0))
    return pl.pallas_call(
        functools.partial(_ffn_body, final_norm=final_norm),
        out_shape=jax.ShapeDtypeStruct((t, d), F32),
        grid=(t // tm,),
        in_specs=[row, _const_spec((1, d)), _const_spec((d, ff)), _const_spec((d, ff)),
                  _const_spec((ff, d)), _const_spec((1, d))],
        out_specs=row,
        compiler_params=_grid_params(1),
        name="ffn",
    )(x, nw, wg, wu, wd, fw)


def _rope_body(post_ref, fcol_ref, cos_ref, sin_ref, cost_ref, sint_ref):
    ang = fcol_ref[...] * post_ref[0].astype(F32)
    c = jnp.cos(ang)
    s = jnp.sin(ang)
    tm = ang.shape[1]
    pad = HEAD_LANES - MLA_NOPE - MLA_ROPE
    cost = jnp.concatenate([jnp.ones((MLA_NOPE, tm), F32), c, c, jnp.ones((pad, tm), F32)], axis=0)
    sint = jnp.concatenate([jnp.zeros((MLA_NOPE, tm), F32), -s, s, jnp.zeros((pad, tm), F32)], axis=0)
    cost_ref[0] = cost
    sint_ref[0] = sint
    cos_ref[...] = cost.T
    sin_ref[...] = sint.T


def _rope_tables(positions, tm):
    b, s = positions.shape
    t = b * s
    inv_freq = jnp.power(ROPE_THETA, -jnp.arange(0, MLA_ROPE, 2, dtype=F32) / MLA_ROPE)
    nsteps = s // tm
    return pl.pallas_call(
        _rope_body,
        out_shape=(jax.ShapeDtypeStruct((t, HEAD_LANES), F32), jax.ShapeDtypeStruct((t, HEAD_LANES), F32),
                   jax.ShapeDtypeStruct((b, HEAD_LANES, s), F32), jax.ShapeDtypeStruct((b, HEAD_LANES, s), F32)),
        grid=(b, nsteps),
        in_specs=[pl.BlockSpec((1, 1, tm), lambda bi, i: (bi, 0, i)), _const_spec((MLA_ROPE // 2, 1))],
        out_specs=(pl.BlockSpec((tm, HEAD_LANES), lambda bi, i: (bi * nsteps + i, 0)),
                   pl.BlockSpec((tm, HEAD_LANES), lambda bi, i: (bi * nsteps + i, 0)),
                   pl.BlockSpec((1, HEAD_LANES, tm), lambda bi, i: (bi, 0, i)),
                   pl.BlockSpec((1, HEAD_LANES, tm), lambda bi, i: (bi, 0, i))),
        compiler_params=_grid_params(2),
        name="rope_tables",
    )(positions.reshape(b, 1, s), inv_freq[:, None])


GDN_W = GDN_HEADS * GDN_D
SEG_QKV = (0, 3 * GDN_W)
SEG_Z = (SEG_QKV[1], SEG_QKV[1] + GDN_W)
SEG_BA = (SEG_Z[1], SEG_Z[1] + 128)
SEG_CQ = (SEG_BA[1], SEG_BA[1] + MLA_Q_LORA)
SEG_CKV = (SEG_CQ[1], SEG_CQ[1] + MLA_KV_LORA)
SEG_KRM = (SEG_CKV[1], SEG_CKV[1] + HEAD_LANES)
SEG_KRP = (SEG_KRM[1], SEG_KRM[1] + HEAD_LANES)
SEG_GATE = (SEG_KRP[1], SEG_KRP[1] + 2048)
N_BA = 4 * GDN_HEADS


def _inproj_body(x_ref, nw_ref, w_ref, wbat_ref, alog_ref, dtb_ref, alogt_ref, dtbt_ref, qnw_ref, kvnw_ref,
                 cos_ref, sin_ref, qkv_ref, sz_ref, bg_ref, gt_ref, cqn_ref, ckvn_ref, kr_ref, sg_ref):
    hb = _rms(x_ref[...], nw_ref[...]).astype(BF16)

    def seg(bounds):
        return _dot(hb, w_ref[:, bounds[0]:bounds[1]])

    qkv_ref[...] = seg(SEG_QKV)
    sz_ref[...] = _silu(seg(SEG_Z)).astype(BF16)
    ba = seg(SEG_BA)
    lane = lax.broadcasted_iota(jnp.int32, ba.shape, 1)
    decay = -jnp.exp(alog_ref[...]) * _softplus(ba + dtb_ref[...])
    bg_ref[...] = jnp.where(lane < 2 * GDN_HEADS, jax.nn.sigmoid(ba), decay)[:, :N_BA]
    bat = lax.dot_general(wbat_ref[...], hb, NT_DIMS, preferred_element_type=F32)
    at = bat[2 * GDN_HEADS:, :]
    gt_ref[...] = -jnp.exp(alogt_ref[...]) * _softplus(at + dtbt_ref[...])
    cqn_ref[...] = _rms(seg(SEG_CQ), qnw_ref[...]).astype(BF16)
    ckvn_ref[...] = _rms(seg(SEG_CKV), kvnw_ref[...]).astype(BF16)
    kr_ref[...] = seg(SEG_KRM) * cos_ref[...] + seg(SEG_KRP) * sin_ref[...]
    sg_ref[...] = jax.nn.sigmoid(seg(SEG_GATE)).astype(BF16)


def _inproj(x, nw, w, wbat, alog, dtb, alogt, dtbt, qnw, kvnw, cos, sin, *, tm):
    t, d = x.shape
    row = lambda n: pl.BlockSpec((tm, n), lambda i: (i, 0))
    out_shape = (
        jax.ShapeDtypeStruct((t, 3 * GDN_W), F32),
        jax.ShapeDtypeStruct((t, GDN_W), BF16),
        jax.ShapeDtypeStruct((t, N_BA), F32),
        jax.ShapeDtypeStruct((2 * GDN_HEADS, t), F32),
        jax.ShapeDtypeStruct((t, MLA_Q_LORA), BF16),
        jax.ShapeDtypeStruct((t, MLA_KV_LORA), BF16),
        jax.ShapeDtypeStruct((t, HEAD_LANES), F32),
        jax.ShapeDtypeStruct((t, 2048), BF16),
    )
    out_specs = (row(3 * GDN_W), row(GDN_W), row(N_BA), pl.BlockSpec((2 * GDN_HEADS, tm), lambda i: (0, i)),
                 row(MLA_Q_LORA), row(MLA_KV_LORA), row(HEAD_LANES), row(2048))
    return pl.pallas_call(
        _inproj_body,
        out_shape=out_shape,
        grid=(t // tm,),
        in_specs=[row(d), _const_spec((1, d)), _const_spec(w.shape), _const_spec(wbat.shape),
                  _const_spec((1, 128)), _const_spec((1, 128)),
                  _const_spec((2 * GDN_HEADS, 1)), _const_spec((2 * GDN_HEADS, 1)),
                  _const_spec((1, MLA_Q_LORA)), _const_spec((1, MLA_KV_LORA)),
                  row(HEAD_LANES), row(HEAD_LANES)],
        out_specs=out_specs,
        compiler_params=_grid_params(1),
        name="inproj",
    )(x, nw, w, wbat, alog, dtb, alogt, dtbt, qnw, kvnw, cos, sin)


def _mla_prep_body(cqn_ref, ckvn_ref, kr_ref, cost_ref, sint_ref, wqm_ref, wqp_ref, wk_ref, wvt_ref, ones_ref,
                   qt_ref, k_ref, vt_ref):
    cqn = cqn_ref[...]
    qm = lax.dot_general(wqm_ref[...], cqn, NT_DIMS, preferred_element_type=F32)
    qp = lax.dot_general(wqp_ref[...], cqn, NT_DIMS, preferred_element_type=F32)
    qscale = (MLA_NOPE + MLA_ROPE) ** -0.5 * LOG2E
    cost = cost_ref[0] * qscale
    sint = sint_ref[0] * qscale
    ckvn = ckvn_ref[...]
    km = _dot(ckvn, wk_ref[...])
    kr = kr_ref[...]
    for h in range(MLA_HEADS):
        grp = slice(h * HEAD_LANES, (h + 1) * HEAD_LANES)
        qt_ref[0, h, 0] = (qm[grp, :] * cost + qp[grp, :] * sint).astype(BF16)
        k_ref[0, h] = (km[:, grp] + kr).astype(BF16)
    vt = lax.dot_general(wvt_ref[...], ckvn, NT_DIMS, preferred_element_type=F32)
    vt_ref[0, 0] = (vt + ones_ref[...]).astype(BF16)


def _mla_prep(cqn, ckvn, kr, cost, sint, wqm, wqp, wk, wvt, ones, *, b, s, tq, tk):
    nsteps = s // tq
    per_key_block = tk // tq
    vr = MLA_HEADS * V_ROWS
    row = lambda n: pl.BlockSpec((tq, n), lambda bi, i: (bi * nsteps + i, 0))
    tr = pl.BlockSpec((1, HEAD_LANES, tq), lambda bi, i: (bi, 0, i))
    return pl.pallas_call(
        _mla_prep_body,
        out_shape=(jax.ShapeDtypeStruct((b, MLA_HEADS, nsteps, HEAD_LANES, tq), BF16),
                   jax.ShapeDtypeStruct((b, MLA_HEADS, s, HEAD_LANES), BF16),
                   jax.ShapeDtypeStruct((b, s // tk, vr, tk), BF16)),
        grid=(b, nsteps),
        in_specs=[row(MLA_Q_LORA), row(MLA_KV_LORA), row(HEAD_LANES), tr, tr,
                  _const_spec(wqm.shape), _const_spec(wqp.shape), _const_spec(wk.shape), _const_spec(wvt.shape),
                  _const_spec((vr, 1))],
        out_specs=(pl.BlockSpec((1, MLA_HEADS, 1, HEAD_LANES, tq), lambda bi, i: (bi, 0, i, 0, 0)),
                   pl.BlockSpec((1, MLA_HEADS, tq, HEAD_LANES), lambda bi, i: (bi, 0, i, 0)),
                   pl.BlockSpec((1, 1, vr, tq), lambda bi, i: (bi, i // per_key_block, 0, i % per_key_block))),
        compiler_params=_grid_params(2),
        name="mla_prep",
    )(cqn, ckvn, kr, cost, sint, wqm, wqp, wk, wvt, ones)


ATTN_GENS = 3
KEY_BLOCKS_PER_TILE = 2
HEADS_PER_STEP = 2
ATTN_UNROLL = 4
ATTN_LAG = 2
KEY_SLICE = 128
PV_DEPTH = 256


def _attn_body(qt_ref, k_ref, vt_ref, ot_ref, s_buf, p_buf, bm_buf, a_buf, m_ref, acc_ref, *, nblk):
    nq = qt_ref.shape[2]
    tk = p_buf.shape[1]
    per_head = nq * nblk
    nblocks = qt_ref.shape[1] * per_head
    nslices = tk // KEY_SLICE
    pv_every = PV_DEPTH // KEY_SLICE
    m_ref[...] = jnp.full(m_ref.shape, NEG_BIG, F32)
    acc_ref[...] = jnp.zeros(acc_ref.shape, F32)

    def step(t, phase, do_scores=True, do_softmax=True, do_pv=True):
        g_s, g_m, g_p = phase, (phase + 1) % ATTN_GENS, (phase + 2) % ATTN_GENS
        n_m, n_p = t - ATTN_LAG, t - 2 * ATTN_LAG
        if do_scores:
            j_s = t % nblk
            start = j_s * tk if isinstance(j_s, int) else pl.multiple_of(j_s * tk, tk)
            h_s = t // per_head
            qt = qt_ref[0, h_s, (t // nblk) % nq]
        if do_softmax:
            m_old = jnp.where(n_m % nblk == 0, NEG_BIG, m_ref[...])
            m_new = jnp.maximum(m_old, bm_buf[g_m])
            a_buf[g_m] = jnp.exp2(m_old - m_new)
            m_ref[...] = m_new
        if do_pv:
            acc = acc_ref[...] * a_buf[g_p]
            h_p = n_p // per_head
            v0 = h_p * V_ROWS if isinstance(h_p, int) else pl.multiple_of(h_p * V_ROWS, V_ROWS)
            vt = vt_ref.at[0, n_p % nblk, pl.ds(v0, V_ROWS), :]
        for c in range(nslices):
            rows = slice(c * KEY_SLICE, (c + 1) * KEY_SLICE)
            if do_pv and c % pv_every == pv_every - 1:
                deep = slice((c + 1 - pv_every) * KEY_SLICE, (c + 1) * KEY_SLICE)
                acc = acc + _dot(vt[:, deep], p_buf[g_p, deep, :])
            if do_softmax:
                p_buf[g_m, rows, :] = jnp.exp2(s_buf[g_m, rows, :] - m_new).astype(BF16)
            if do_scores and c == 0:
                s = _dot(k_ref[0, h_s, pl.ds(start, tk), :], qt)
                s_buf[g_s] = s
                bm_buf[g_s] = jnp.max(s, axis=0, keepdims=True)
        if do_pv:
            acc_ref[...] = acc
            return (h_p, (n_p // nblk) % nq), acc
        return None

    def write(out):
        if out is not None:
            ot_ref[0, out[0][0], out[0][1]] = out[1]

    def static_step(t):
        write(step(t, t % ATTN_GENS, do_scores=t < nblocks, do_softmax=ATTN_LAG <= t < nblocks + ATTN_LAG,
                   do_pv=t >= 2 * ATTN_LAG))

    fill = 2 * ATTN_LAG
    for t in range(fill):
        static_step(t)
    per_iter = ATTN_GENS * ATTN_UNROLL
    nloop = max(nblocks - fill, 0) // per_iter

    def body(u, carry):
        t0 = fill + per_iter * u
        outs = [step(t0 + i, (fill + i) % ATTN_GENS) for i in range(per_iter)]
        for out in outs:
            write(out)
        return carry

    lax.fori_loop(0, nloop, body, 0)
    for t in range(fill + per_iter * nloop, nblocks + fill):
        static_step(t)


def _attention(qt, k, vt, *, b, s, tq, tk):
    nq = s // tq
    nk = s // tk
    assert tk % PV_DEPTH == 0 and PV_DEPTH % KEY_SLICE == 0
    hps = HEADS_PER_STEP
    return pl.pallas_call(
        functools.partial(_attn_body, nblk=nk),
        out_shape=jax.ShapeDtypeStruct((b, MLA_HEADS, nq, V_ROWS, tq), F32),
        grid=(b, MLA_HEADS // hps),
        in_specs=[pl.BlockSpec((1, hps, nq, HEAD_LANES, tq), lambda bi, h: (bi, h, 0, 0, 0)),
                  pl.BlockSpec((1, hps, s, HEAD_LANES), lambda bi, h: (bi, h, 0, 0)),
                  pl.BlockSpec((1, nk, hps * V_ROWS, tk), lambda bi, h: (bi, 0, h, 0))],
        out_specs=pl.BlockSpec((1, hps, nq, V_ROWS, tq), lambda bi, h: (bi, h, 0, 0, 0)),
        scratch_shapes=[pltpu.VMEM((ATTN_GENS, tk, tq), F32), pltpu.VMEM((ATTN_GENS, tk, tq), BF16),
                        pltpu.VMEM((ATTN_GENS, 1, tq), F32), pltpu.VMEM((ATTN_GENS, 1, tq), F32),
                        pltpu.VMEM((1, tq), F32), pltpu.VMEM((V_ROWS, tq), F32)],
        compiler_params=_grid_params(2),
        name="attention",
    )(qt, k, vt)


HALO = 8


def _gdn_prep_body(prev_ref, cur_ref, next_ref, cw_ref, q_ref, k_ref, v_ref, buf_ref, *, tiles_per_seq):
    i = pl.program_id(0)
    tm = cur_ref.shape[0]
    first = (i % tiles_per_seq) == 0
    last = (i % tiles_per_seq) == tiles_per_seq - 1
    buf_ref[0:HALO, :] = jnp.where(first, 0.0, prev_ref[...])
    buf_ref[HALO:HALO + tm, :] = cur_ref[...]
    buf_ref[HALO + tm:2 * HALO + tm, :] = jnp.where(last, 0.0, next_ref[...])
    pad = GDN_CONV // 2
    for grp in range(3 * GDN_HEADS):
        lanes = slice(grp * GDN_D, (grp + 1) * GDN_D)
        acc = None
        for j in range(GDN_CONV):
            lo = HALO - pad + j
            term = buf_ref[lo:lo + tm, lanes] * cw_ref[j:j + 1, lanes]
            acc = term if acc is None else acc + term
        y = _silu(acc)
        if grp < 2 * GDN_HEADS:
            y = y * lax.rsqrt(jnp.sum(y * y, axis=-1, keepdims=True) + EPS)
        if grp < GDN_HEADS:
            q_ref[:, lanes] = y * GDN_D ** -0.5
        elif grp < 2 * GDN_HEADS:
            k_ref[:, slice(lanes.start - GDN_W, lanes.stop - GDN_W)] = y
        else:
            v_ref[:, slice(lanes.start - 2 * GDN_W, lanes.stop - 2 * GDN_W)] = y


def _gdn_prep(qkv, conv_w, *, s, tm):
    t, c = qkv.shape
    tiles_per_seq = s // tm
    hb = tm // HALO
    nh = t // HALO
    out = jax.ShapeDtypeStruct((t, GDN_W), F32)
    row = pl.BlockSpec((tm, GDN_W), lambda i: (i, 0))
    return pl.pallas_call(
        functools.partial(_gdn_prep_body, tiles_per_seq=tiles_per_seq),
        out_shape=(out, out, out),
        grid=(t // tm,),
        in_specs=[pl.BlockSpec((HALO, c), lambda i: (jnp.maximum(i * hb - 1, 0), 0)),
                  pl.BlockSpec((tm, c), lambda i: (i, 0)),
                  pl.BlockSpec((HALO, c), lambda i: (jnp.minimum((i + 1) * hb, nh - 1), 0)),
                  _const_spec(conv_w.shape)],
        out_specs=(row, row, row),
        scratch_shapes=[pltpu.VMEM((tm + 2 * HALO, c), F32)],
        compiler_params=_grid_params(1),
        name="gdn_prep",
    )(qkv, qkv, qkv, conv_w)


CH = GDN_CHUNK
HC = GDN_HEADS * CH
SCAN_CHUNKS = 2


def _split3(x):
    hi = x.astype(BF16)
    r = x - hi.astype(F32)
    mid = r.astype(BF16)
    lo = (r - mid.astype(F32)).astype(BF16)
    return hi, mid, lo


def _bcast_cols(cols, width):
    if width == 128:
        return jnp.concatenate([jnp.broadcast_to(c, (CH, 128)) for c in cols], axis=1)
    lane = lax.broadcasted_iota(jnp.int32, (CH, 128), 1)
    lo = lane < 64
    pair = lambda a, b: jnp.where(lo, jnp.broadcast_to(a, (CH, 128)), jnp.broadcast_to(b, (CH, 128)))
    return jnp.concatenate([pair(cols[0], cols[1]), pair(cols[2], cols[3])], axis=1)


def _dir_masks(d):
    r64 = lax.broadcasted_iota(jnp.int32, (CH, CH), 0)
    c64 = lax.broadcasted_iota(jnp.int32, (CH, CH), 1)
    r256 = lax.broadcasted_iota(jnp.int32, (CH, HC), 0)
    c256 = lax.broadcasted_iota(jnp.int32, (CH, HC), 1) & (CH - 1)
    if d == 0:
        tri, incl, strict, trit, last = (c64 <= r64), (c256 <= r256), (c256 < r256), (r256 <= c256), CH - 1
    else:
        tri, incl, strict, trit, last = (c64 >= r64), (c256 >= r256), (c256 > r256), (r256 >= c256), 0
    tri3 = jnp.concatenate([tri.astype(BF16)] * 3, axis=1)
    trit3 = jnp.concatenate([trit.astype(BF16)] * 3, axis=0)
    return dict(tri3=tri3, trit3=trit3, incl=incl, strict=strict, last=last)


def _pair_diag(a, b):
    z = jnp.zeros(a.shape, a.dtype)
    return jnp.concatenate([jnp.concatenate([a, z], axis=1), jnp.concatenate([z, b], axis=1)], axis=0)


def _gdn_scan_body(qf_ref, kf_ref, vf_ref, bgf_ref, gtf_ref, qb_ref, kb_ref, vb_ref, bgb_ref, gtb_ref,
                   of_ref, ob_ref, s_ref):
    @pl.when(pl.program_id(0) == 0)
    def _():
        s_ref[...] = jnp.zeros(s_ref.shape, F32)

    nb = qf_ref.shape[0]
    ncs = qf_ref.shape[1] // CH
    in_refs = ((qf_ref, kf_ref, vf_ref, bgf_ref, gtf_ref), (qb_ref, kb_ref, vb_ref, bgb_ref, gtb_ref))
    out_refs = (of_ref, ob_ref)
    masks = (_dir_masks(0), _dir_masks(1))
    groups = [(d, b, r if d == 0 else ncs - 1 - r) for r in range(ncs) for b in range(nb) for d in (0, 1)]
    per_rank = 2 * nb

    def rows(d, idx, b, c):
        return in_refs[d][idx][b, c * CH:(c + 1) * CH, :]
    heads = range(GDN_HEADS)
    pairs = range(GDN_HEADS // 2)

    r256 = lax.broadcasted_iota(jnp.int32, (CH, HC), 0)
    c256 = lax.broadcasted_iota(jnp.int32, (CH, HC), 1) & (CH - 1)
    eye = (c256 == r256).astype(F32)
    bd_mask = ((lax.broadcasted_iota(jnp.int32, (HC, HC), 0) >> 6)
               == (lax.broadcasted_iota(jnp.int32, (HC, HC), 1) >> 6)).astype(BF16)
    kbd_mask = ((lax.broadcasted_iota(jnp.int32, (HC, GDN_W), 0) >> 6)
                == (lax.broadcasted_iota(jnp.int32, (HC, GDN_W), 1) >> 7)).astype(BF16)
    r8 = lax.broadcasted_iota(jnp.int32, (2 * GDN_HEADS, HC), 0)
    h8 = lax.broadcasted_iota(jnp.int32, (2 * GDN_HEADS, HC), 1) >> 6

    st = []
    for d, b, c in groups:
        mk = masks[d]
        bg = rows(d, 3, b, c)
        gt = in_refs[d][4][b, c]
        cs = _dot(mk["tri3"], jnp.concatenate(_split3(bg), axis=0))
        cst = _dot(jnp.concatenate(_split3(gt), axis=1), mk["trit3"])
        st.append(dict(bg=bg, cs=cs, cst=cst))

    for g, (d, b, c) in enumerate(groups):
        mk, e = masks[d], st[g]
        crow = jnp.sum(jnp.where(r8 == d * GDN_HEADS + h8, e["cst"], 0.0), axis=0, keepdims=True)
        goff = 2 * GDN_HEADS + d * GDN_HEADS
        ccols = [e["cs"][:, goff + h:goff + h + 1] for h in heads]
        betas = [e["bg"][:, d * GDN_HEADS + h:d * GDN_HEADS + h + 1] for h in heads]
        e["ccols"] = ccols
        e["ccol512"] = _bcast_cols(ccols, GDN_D)
        e["beta512"] = _bcast_cols(betas, GDN_D)
        e["decay"] = jnp.exp(jnp.where(mk["incl"], _bcast_cols(ccols, CH) - crow, -jnp.inf))

    for g, (d, b, c) in enumerate(groups):
        mk, e = masks[d], st[g]
        q = rows(d, 0, b, c)
        k = rows(d, 1, b, c)
        kb = k * e["beta512"]
        kbd = jnp.concatenate([k.astype(BF16)] * GDN_HEADS, axis=0) * kbd_mask
        kq = lax.dot_general(jnp.concatenate([kb, q], axis=0).astype(BF16), kbd, NT_DIMS,
                             preferred_element_type=F32)
        neg_l = jnp.where(mk["strict"], -(kq[:CH] * e["decay"]), 0.0)
        e["intra"] = (kq[CH:] * e["decay"]).astype(BF16)
        e["p"] = eye + neg_l
        e["lm"] = neg_l

    for level in range(6):
        for e in st:
            w_bd = jnp.concatenate([e["lm"].astype(BF16)] * GDN_HEADS, axis=0) * bd_mask
            if level == 0:
                e["lm"] = _dot(e["lm"].astype(BF16), w_bd)
            elif level < 5:
                y = _dot(jnp.concatenate([e["p"], e["lm"]], axis=0).astype(BF16), w_bd)
                e["p"] = e["p"] + y[:CH]
                e["lm"] = y[CH:]
            else:
                e["p"] = e["p"] + _dot(e["p"].astype(BF16), w_bd)

    for g, (d, b, c) in enumerate(groups):
        e = st[g]
        k = rows(d, 1, b, c)
        v = rows(d, 2, b, c)
        tinv = e["p"].astype(BF16)
        vb = (v * e["beta512"]).astype(BF16)
        kbe = (k * e["beta512"] * jnp.exp(e["ccol512"])).astype(BF16)
        e["uw"] = []
        for pr in pairs:
            h0, h1 = 2 * pr, 2 * pr + 1
            rhs = lambda h: jnp.concatenate([vb[:, h * GDN_D:(h + 1) * GDN_D], kbe[:, h * GDN_D:(h + 1) * GDN_D]], axis=1)
            e["uw"].append(_dot(tinv[:, pr * GDN_D:(pr + 1) * GDN_D], _pair_diag(rhs(h0), rhs(h1))))

    for r in range(ncs):
        _gdn_state_stage(st, groups, r * per_rank, (r + 1) * per_rank, rows, out_refs, masks, s_ref)


def _gdn_state_stage(st, groups, lo, hi, rows, out_refs, masks, s_ref):
    pairs = range(GDN_HEADS // 2)
    for g in range(lo, hi):
        d, b, c = groups[g]
        e = st[g]
        q = rows(d, 0, b, c)
        qe = q * jnp.exp(e["ccol512"])
        e["ws"] = []
        for pr in pairs:
            lhs = []
            for h in (2 * pr, 2 * pr + 1):
                w = e["uw"][pr][:, (2 * (h % 2) + 1) * GDN_D:(2 * (h % 2) + 2) * GDN_D]
                lhs.append(jnp.concatenate([w, qe[:, h * GDN_D:(h + 1) * GDN_D]], axis=0))
            sidx = (b * 2 + d) * GDN_HEADS + 2 * pr
            sbd = _pair_diag(s_ref[sidx].astype(BF16), s_ref[sidx + 1].astype(BF16))
            e["ws"].append(_dot(jnp.concatenate(lhs, axis=1).astype(BF16), sbd))

    for g in range(lo, hi):
        d, b, c = groups[g]
        e = st[g]
        k = rows(d, 1, b, c)
        last = masks[d]["last"]
        for pr in pairs:
            vnew = []
            for h in (2 * pr, 2 * pr + 1):
                u = e["uw"][pr][:, 2 * (h % 2) * GDN_D:(2 * (h % 2) + 1) * GDN_D]
                vnew.append((u - e["ws"][pr][:CH, (h % 2) * GDN_D:(h % 2 + 1) * GDN_D]).astype(BF16))
            glast = [e["ccols"][h][last:last + 1, :] for h in (2 * pr, 2 * pr + 1)]
            kd = [k[:, h * GDN_D:(h + 1) * GDN_D] * jnp.exp(gl - e["ccol512"][:, h * GDN_D:(h + 1) * GDN_D])
                  for h, gl in zip((2 * pr, 2 * pr + 1), glast)]
            kdt = jnp.concatenate(kd, axis=0).T.astype(BF16)
            lhs = jnp.concatenate([e["intra"][:, pr * GDN_D:(pr + 1) * GDN_D], kdt], axis=0)
            prod = _dot(lhs, _pair_diag(vnew[0], vnew[1]))
            out_refs[d][b, c * CH:(c + 1) * CH, 2 * pr * GDN_D:(2 * pr + 2) * GDN_D] = (
                e["ws"][pr][CH:, :] + prod[:CH]).astype(BF16)
            for i, h in enumerate((2 * pr, 2 * pr + 1)):
                sidx = (b * 2 + d) * GDN_HEADS + h
                s_ref[sidx] = s_ref[sidx] * jnp.exp(glast[i]) + prod[CH:, i * GDN_D:(i + 1) * GDN_D]


def _gdn_scan(q, k, v, bg, gt, *, b, s):
    ncs = SCAN_CHUNKS if (s // CH) % SCAN_CHUNKS == 0 else 1
    n = s // (CH * ncs)
    fwd = lambda i: (0, i, 0)
    bwd = lambda i: (0, n - 1 - i, 0)
    fwd_t = lambda i: (0, i, 0, 0)
    bwd_t = lambda i: (0, n - 1 - i, 0, 0)
    wide = lambda im: pl.BlockSpec((b, ncs * CH, GDN_W), im)
    specs = lambda im, imt: [wide(im), wide(im), wide(im), pl.BlockSpec((b, ncs * CH, N_BA), im),
                             pl.BlockSpec((b, ncs, 2 * GDN_HEADS, CH), imt)]
    out = jax.ShapeDtypeStruct((b, s, GDN_W), BF16)
    return pl.pallas_call(
        _gdn_scan_body,
        out_shape=(out, out),
        grid=(n,),
        in_specs=specs(fwd, fwd_t) + specs(bwd, bwd_t),
        out_specs=(wide(fwd), wide(bwd)),
        scratch_shapes=[pltpu.VMEM((b * 2 * GDN_HEADS, GDN_D, GDN_D), F32)],
        compiler_params=_grid_params(1),
        name="gdn_scan",
    )(q, k, v, bg, gt, q, k, v, bg, gt)


def _merge_body(x_ref, of_ref, ob_ref, sz_ref, sg_ref, ot_ref, gnw_ref, wgp_ref, wmp_ref, wo_ref, o_ref):
    o = of_ref[...].astype(F32) + ob_ref[...].astype(F32)
    gnw = gnw_ref[...]
    heads = []
    for h in range(GDN_HEADS):
        oh = o[:, h * GDN_D:(h + 1) * GDN_D]
        heads.append(_rms(oh, gnw))
    on = jnp.concatenate(heads, axis=1) * sz_ref[...]
    ya = _dot(on.astype(BF16), wgp_ref[...])
    acc_t = ot_ref[0, :, 0]
    ot = (acc_t[:, :MLA_V, :] / acc_t[:, MLA_V:MLA_V + 1, :]).reshape(MLA_HEADS * MLA_V, -1)
    yb = _dot(ot.T.astype(BF16), wmp_ref[...])
    d = ya.shape[1]
    y = sg_ref[:, :d] * ya + sg_ref[:, d:] * yb
    o_ref[...] = x_ref[...] + _dot(y.astype(BF16), wo_ref[...])


def _merge(x, of, ob, sz, sg, ot, gnw, wgp, wmp, wo, *, b, s, tm):
    t, d = x.shape
    nsteps = s // tm
    row = lambda n: pl.BlockSpec((tm, n), lambda bi, i: (bi * nsteps + i, 0))
    return pl.pallas_call(
        _merge_body,
        out_shape=jax.ShapeDtypeStruct((t, d), F32),
        grid=(b, nsteps),
        in_specs=[row(d), row(GDN_W), row(GDN_W), row(GDN_W), row(2 * d),
                  pl.BlockSpec((1, MLA_HEADS, 1, V_ROWS, tm), lambda bi, i: (bi, 0, i, 0, 0)),
                  _const_spec((1, GDN_D)), _const_spec(wgp.shape), _const_spec(wmp.shape), _const_spec(wo.shape)],
        out_specs=row(d),
        compiler_params=_grid_params(2),
        name="merge",
    )(x, of, ob, sz, sg, ot, gnw, wgp, wmp, wo)


def _pack_w_in(w):
    d = w.shape[0]
    zeros = lambda n: jnp.zeros((d, n), w.dtype)
    o = 4 * GDN_W
    ba = w[:, o:o + N_BA]
    o += N_BA
    cq = w[:, o:o + MLA_Q_LORA]
    o += MLA_Q_LORA
    ckv = w[:, o:o + MLA_KV_LORA]
    o += MLA_KV_LORA
    kr = w[:, o:o + MLA_ROPE]
    o += MLA_ROPE
    gates = w[:, o:]
    half = MLA_ROPE // 2
    tail = zeros(HEAD_LANES - MLA_NOPE - MLA_ROPE)
    kr_main = jnp.concatenate([zeros(MLA_NOPE), kr, tail], axis=1)
    kr_swap = jnp.concatenate([zeros(MLA_NOPE), kr[:, half:], kr[:, :half], tail], axis=1)
    packed = jnp.concatenate([w[:, :4 * GDN_W], ba, zeros(128 - N_BA), cq, ckv, kr_main, kr_swap, gates], axis=1)
    return packed.astype(BF16), ba.T.astype(BF16)


def _pack_w_uq(w):
    r = w.shape[0]
    qk = MLA_NOPE + MLA_ROPE
    half = MLA_ROPE // 2
    tail = jnp.zeros((r, HEAD_LANES - qk), w.dtype)
    znope = jnp.zeros((r, MLA_NOPE), w.dtype)
    main, swap = [], []
    for h in range(MLA_HEADS):
        nope = w[:, h * qk:h * qk + MLA_NOPE]
        rope = w[:, h * qk + MLA_NOPE:(h + 1) * qk]
        main += [nope, rope, tail]
        swap += [znope, rope[:, half:], rope[:, :half], tail]
    return jnp.concatenate(main, axis=1).T.astype(BF16), jnp.concatenate(swap, axis=1).T.astype(BF16)


def _pack_w_ukv(w):
    r = w.shape[0]
    hw = MLA_NOPE + MLA_V
    ks, vs = [], []
    for h in range(MLA_HEADS):
        ks += [w[:, h * hw:h * hw + MLA_NOPE], jnp.zeros((r, HEAD_LANES - MLA_NOPE), w.dtype)]
        vs += [w[:, h * hw + MLA_NOPE:(h + 1) * hw], jnp.zeros((r, V_ROWS - MLA_V), w.dtype)]
    return jnp.concatenate(ks, axis=1).astype(BF16), jnp.concatenate(vs, axis=1).T.astype(BF16)


def _ones_rows():
    idx = jnp.arange(MLA_HEADS * V_ROWS) % V_ROWS
    return (idx == MLA_V).astype(F32)[:, None]


def _lane_pad(v, lo, width):
    return jnp.zeros((1, width), v.dtype).at[0, lo:lo + v.shape[0]].set(v)


def kernel(x, positions, norm_ffn1, ffn1_w_gate, ffn1_w_up, ffn1_w_down, norm_mix, w_in, gdn_conv, gdn_A_log,
           gdn_dt_bias, gdn_norm, gdn_proj, mla_q_norm, mla_w_uq, mla_kv_norm, mla_w_ukv, mla_proj, w_out,
           norm_ffn2, ffn2_w_gate, ffn2_w_up, ffn2_w_down, final_norm):
    b, s, d = x.shape
    t = b * s
    depth = w_in.shape[0]
    tm = 512 if s % 2048 == 0 else 256
    ffn_tm = 512
    tq = tm
    tk = KEY_BLOCKS_PER_TILE * tm

    cos, sin, cost, sint = _rope_tables(positions, tm)
    ones = _ones_rows()
    xf = x.reshape(t, d)
    row = lambda v: v.reshape(1, -1)
    for l in range(depth):
        xf = _ffn(xf, row(norm_ffn1[l]), ffn1_w_gate[l].astype(BF16), ffn1_w_up[l].astype(BF16),
                  ffn1_w_down[l].astype(BF16), row(final_norm), final_norm=False, tm=ffn_tm)

        w_packed, w_bat = _pack_w_in(w_in[l])
        alog = gdn_A_log[l].reshape(-1)
        dtb = gdn_dt_bias[l].reshape(-1)
        qkv, sz, bg, gt, cqn, ckvn, kr, sg = _inproj(
            xf, row(norm_mix[l]), w_packed, w_bat, _lane_pad(alog, 2 * GDN_HEADS, 128),
            _lane_pad(dtb, 2 * GDN_HEADS, 128), alog[:, None], dtb[:, None], row(mla_q_norm[l]),
            row(mla_kv_norm[l]), cos, sin, tm=tm)

        qn, kn, vv = _gdn_prep(qkv, gdn_conv[l], s=s, tm=tm)
        seq = lambda a: a.reshape(b, s, a.shape[-1])
        gt_chunks = gt.reshape(2 * GDN_HEADS, b, s // CH, CH).transpose(1, 2, 0, 3)
        of, ob = _gdn_scan(seq(qn), seq(kn), seq(vv), seq(bg), gt_chunks, b=b, s=s)
        of, ob = of.reshape(t, GDN_W), ob.reshape(t, GDN_W)

        wqm, wqp = _pack_w_uq(mla_w_uq[l])
        wk, wvt = _pack_w_ukv(mla_w_ukv[l])
        qt, kk, vt = _mla_prep(cqn, ckvn, kr, cost, sint, wqm, wqp, wk, wvt, ones, b=b, s=s, tq=tq, tk=tk)
        ot = _attention(qt, kk, vt, b=b, s=s, tq=tq, tk=tk)

        xf = _merge(xf, of, ob, sz, sg, ot, row(gdn_norm[l]), gdn_proj[l].astype(BF16), mla_proj[l].astype(BF16),
                    w_out[l].astype(BF16), b=b, s=s, tm=tm)

        xf = _ffn(xf, row(norm_ffn2[l]), ffn2_w_gate[l].astype(BF16), ffn2_w_up[l].astype(BF16),
                  ffn2_w_down[l].astype(BF16), row(final_norm), final_norm=(l == depth - 1), tm=ffn_tm)
    return xf.reshape(b, s, d)
```

```python
import functools

import jax
import jax.numpy as jnp
from jax import lax
from jax.experimental import pallas as pl
from jax.experimental.pallas import tpu as pltpu

F32 = jnp.float32
BF16 = jnp.bfloat16

EPS = 1e-6
RES_HALF = 0.5
GDN_HEADS = 4
GDN_D = 128
GDN_CONV = 5
GDN_CHUNK = 64
MLA_HEADS = 8
MLA_NOPE = 64
MLA_ROPE = 32
MLA_V = 64
MLA_Q_LORA = 384
MLA_KV_LORA = 256
ROPE_THETA = 10000.0
HEAD_LANES = 128
V_ROWS = 80
LOG2E = 1.4426950408889634
NEG_BIG = -1e30

VMEM_LIMIT_BYTES = 56 * 1024 * 1024

NT_DIMS = (((1,), (1,)), ((), ()))


def _grid_params(n, flags=None):
    return pltpu.CompilerParams(dimension_semantics=("arbitrary",) * n, vmem_limit_bytes=VMEM_LIMIT_BYTES,
                                flags=flags)


def _const_spec(shape):
    nd = len(shape)
    return pl.BlockSpec(shape, lambda *_: (0,) * nd, pipeline_mode=pl.Buffered(1))


def _rms(x, w):
    return x * lax.rsqrt(jnp.mean(x * x, axis=-1, keepdims=True) + EPS) * w


def _silu(x):
    return x * jax.nn.sigmoid(x)


def _softplus(x):
    return jnp.maximum(x, 0.0) + jnp.log1p(jnp.exp(-jnp.abs(x)))


def _dot(a, b):
    return jnp.dot(a, b, preferred_element_type=F32)


def _ffn_body(x_ref, nw_ref, wg_ref, wu_ref, wd_ref, fw_ref, o_ref, *, final_norm):
    x = x_ref[...]
    hb = _rms(x, nw_ref[...]).astype(BF16)
    g = _dot(hb, wg_ref[...])
    u = _dot(hb, wu_ref[...])
    a = (_silu(g) * u).astype(BF16)
    y = x + RES_HALF * _dot(a, wd_ref[...])
    if final_norm:
        y = _rms(y, fw_ref[...])
    o_ref[...] = y


def _ffn(x, nw, wg, wu, wd, fw, *, final_norm, tm):
    t, d = x.shape
    ff = wg.shape[1]
    row = pl.BlockSpec((tm, d), lambda i: (i, 0))
    return pl.pallas_call(
        functools.partial(_ffn_body, final_norm=final_norm),
        out_shape=jax.ShapeDtypeStruct((t, d), F32),
        grid=(t // tm,),
        in_specs=[row, _const_spec((1, d)), _const_spec((d, ff)), _const_spec((d, ff)),
                  _const_spec((ff, d)), _const_spec((1, d))],
        out_specs=row,
        compiler_params=_grid_params(1),
        name="ffn",
    )(x, nw, wg, wu, wd, fw)


def _rope_body(post_ref, fcol_ref, cos_ref, sin_ref, cost_ref, sint_ref):
    ang = fcol_ref[...] * post_ref[0].astype(F32)
    c = jnp.cos(ang)
    s = jnp.sin(ang)
    tm = ang.shape[1]
    pad = HEAD_LANES - MLA_NOPE - MLA_ROPE
    cost = jnp.concatenate([jnp.ones((MLA_NOPE, tm), F32), c, c, jnp.ones((pad, tm), F32)], axis=0)
    sint = jnp.concatenate([jnp.zeros((MLA_NOPE, tm), F32), -s, s, jnp.zeros((pad, tm), F32)], axis=0)
    cost_ref[0] = cost
    sint_ref[0] = sint
    cos_ref[...] = cost.T
    sin_ref[...] = sint.T


def _rope_tables(positions, tm):
    b, s = positions.shape
    t = b * s
    inv_freq = jnp.power(ROPE_THETA, -jnp.arange(0, MLA_ROPE, 2, dtype=F32) / MLA_ROPE)
    nsteps = s // tm
    return pl.pallas_call(
        _rope_body,
        out_shape=(jax.ShapeDtypeStruct((t, HEAD_LANES), F32), jax.ShapeDtypeStruct((t, HEAD_LANES), F32),
                   jax.ShapeDtypeStruct((b, HEAD_LANES, s), F32), jax.ShapeDtypeStruct((b, HEAD_LANES, s), F32)),
        grid=(b, nsteps),
        in_specs=[pl.BlockSpec((1, 1, tm), lambda bi, i: (bi, 0, i)), _const_spec((MLA_ROPE // 2, 1))],
        out_specs=(pl.BlockSpec((tm, HEAD_LANES), lambda bi, i: (bi * nsteps + i, 0)),
                   pl.BlockSpec((tm, HEAD_LANES), lambda bi, i: (bi * nsteps + i, 0)),
                   pl.BlockSpec((1, HEAD_LANES, tm), lambda bi, i: (bi, 0, i)),
                   pl.BlockSpec((1, HEAD_LANES, tm), lambda bi, i: (bi, 0, i))),
        compiler_params=_grid_params(2),
        name="rope_tables",
    )(positions.reshape(b, 1, s), inv_freq[:, None])


GDN_W = GDN_HEADS * GDN_D
SEG_QKV = (0, 3 * GDN_W)
SEG_Z = (SEG_QKV[1], SEG_QKV[1] + GDN_W)
SEG_BA = (SEG_Z[1], SEG_Z[1] + 128)
SEG_CQ = (SEG_BA[1], SEG_BA[1] + MLA_Q_LORA)
SEG_CKV = (SEG_CQ[1], SEG_CQ[1] + MLA_KV_LORA)
SEG_KRM = (SEG_CKV[1], SEG_CKV[1] + HEAD_LANES)
SEG_KRP = (SEG_KRM[1], SEG_KRM[1] + HEAD_LANES)
SEG_GATE = (SEG_KRP[1], SEG_KRP[1] + 2048)
N_BA = 4 * GDN_HEADS


def _inproj_body(x_ref, nw_ref, w_ref, wbat_ref, alog_ref, dtb_ref, alogt_ref, dtbt_ref, qnw_ref, kvnw_ref,
                 cos_ref, sin_ref, qkv_ref, sz_ref, bg_ref, gt_ref, cqn_ref, ckvn_ref, kr_ref, sg_ref):
    hb = _rms(x_ref[...], nw_ref[...]).astype(BF16)

    def seg(bounds):
        return _dot(hb, w_ref[:, bounds[0]:bounds[1]])

    qkv_ref[...] = seg(SEG_QKV)
    sz_ref[...] = _silu(seg(SEG_Z)).astype(BF16)
    ba = seg(SEG_BA)
    lane = lax.broadcasted_iota(jnp.int32, ba.shape, 1)
    decay = -jnp.exp(alog_ref[...]) * _softplus(ba + dtb_ref[...])
    bg_ref[...] = jnp.where(lane < 2 * GDN_HEADS, jax.nn.sigmoid(ba), decay)[:, :N_BA]
    bat = lax.dot_general(wbat_ref[...], hb, NT_DIMS, preferred_element_type=F32)
    at = bat[2 * GDN_HEADS:, :]
    gt_ref[...] = -jnp.exp(alogt_ref[...]) * _softplus(at + dtbt_ref[...])
    cqn_ref[...] = _rms(seg(SEG_CQ), qnw_ref[...]).astype(BF16)
    ckvn_ref[...] = _rms(seg(SEG_CKV), kvnw_ref[...]).astype(BF16)
    kr_ref[...] = seg(SEG_KRM) * cos_ref[...] + seg(SEG_KRP) * sin_ref[...]
    sg_ref[...] = jax.nn.sigmoid(seg(SEG_GATE)).astype(BF16)


def _inproj(x, nw, w, wbat, alog, dtb, alogt, dtbt, qnw, kvnw, cos, sin, *, tm):
    t, d = x.shape
    row = lambda n: pl.BlockSpec((tm, n), lambda i: (i, 0))
    out_shape = (
        jax.ShapeDtypeStruct((t, 3 * GDN_W), F32),
        jax.ShapeDtypeStruct((t, GDN_W), BF16),
        jax.ShapeDtypeStruct((t, N_BA), F32),
        jax.ShapeDtypeStruct((2 * GDN_HEADS, t), F32),
        jax.ShapeDtypeStruct((t, MLA_Q_LORA), BF16),
        jax.ShapeDtypeStruct((t, MLA_KV_LORA), BF16),
        jax.ShapeDtypeStruct((t, HEAD_LANES), F32),
        jax.ShapeDtypeStruct((t, 2048), BF16),
    )
    out_specs = (row(3 * GDN_W), row(GDN_W), row(N_BA), pl.BlockSpec((2 * GDN_HEADS, tm), lambda i: (0, i)),
                 row(MLA_Q_LORA), row(MLA_KV_LORA), row(HEAD_LANES), row(2048))
    return pl.pallas_call(
        _inproj_body,
        out_shape=out_shape,
        grid=(t // tm,),
        in_specs=[row(d), _const_spec((1, d)), _const_spec(w.shape), _const_spec(wbat.shape),
                  _const_spec((1, 128)), _const_spec((1, 128)),
                  _const_spec((2 * GDN_HEADS, 1)), _const_spec((2 * GDN_HEADS, 1)),
                  _const_spec((1, MLA_Q_LORA)), _const_spec((1, MLA_KV_LORA)),
                  row(HEAD_LANES), row(HEAD_LANES)],
        out_specs=out_specs,
        compiler_params=_grid_params(1),
        name="inproj",
    )(x, nw, w, wbat, alog, dtb, alogt, dtbt, qnw, kvnw, cos, sin)


def _mla_prep_body(cqn_ref, ckvn_ref, kr_ref, cost_ref, sint_ref, wqm_ref, wqp_ref, wk_ref, wvt_ref, ones_ref,
                   qt_ref, k_ref, vt_ref):
    cqn = cqn_ref[...]
    qm = lax.dot_general(wqm_ref[...], cqn, NT_DIMS, preferred_element_type=F32)
    qp = lax.dot_general(wqp_ref[...], cqn, NT_DIMS, preferred_element_type=F32)
    qscale = (MLA_NOPE + MLA_ROPE) ** -0.5 * LOG2E
    cost = cost_ref[0] * qscale
    sint = sint_ref[0] * qscale
    ckvn = ckvn_ref[...]
    km = _dot(ckvn, wk_ref[...])
    kr = kr_ref[...]
    for h in range(MLA_HEADS):
        grp = slice(h * HEAD_LANES, (h + 1) * HEAD_LANES)
        qt_ref[0, h, 0] = (qm[grp, :] * cost + qp[grp, :] * sint).astype(BF16)
        k_ref[0, h] = (km[:, grp] + kr).astype(BF16)
    vt = lax.dot_general(wvt_ref[...], ckvn, NT_DIMS, preferred_element_type=F32)
    vt_ref[0, 0] = (vt + ones_ref[...]).astype(BF16)


def _mla_prep(cqn, ckvn, kr, cost, sint, wqm, wqp, wk, wvt, ones, *, b, s, tq, tk):
    nsteps = s // tq
    per_key_block = tk // tq
    vr = MLA_HEADS * V_ROWS
    row = lambda n: pl.BlockSpec((tq, n), lambda bi, i: (bi * nsteps + i, 0))
    tr = pl.BlockSpec((1, HEAD_LANES, tq), lambda bi, i: (bi, 0, i))
    return pl.pallas_call(
        _mla_prep_body,
        out_shape=(jax.ShapeDtypeStruct((b, MLA_HEADS, nsteps, HEAD_LANES, tq), BF16),
                   jax.ShapeDtypeStruct((b, MLA_HEADS, s, HEAD_LANES), BF16),
                   jax.ShapeDtypeStruct((b, s // tk, vr, tk), BF16)),
        grid=(b, nsteps),
        in_specs=[row(MLA_Q_LORA), row(MLA_KV_LORA), row(HEAD_LANES), tr, tr,
                  _const_spec(wqm.shape), _const_spec(wqp.shape), _const_spec(wk.shape), _const_spec(wvt.shape),
                  _const_spec((vr, 1))],
        out_specs=(pl.BlockSpec((1, MLA_HEADS, 1, HEAD_LANES, tq), lambda bi, i: (bi, 0, i, 0, 0)),
                   pl.BlockSpec((1, MLA_HEADS, tq, HEAD_LANES), lambda bi, i: (bi, 0, i, 0)),
                   pl.BlockSpec((1, 1, vr, tq), lambda bi, i: (bi, i // per_key_block, 0, i % per_key_block))),
        compiler_params=_grid_params(2),
        name="mla_prep",
    )(cqn, ckvn, kr, cost, sint, wqm, wqp, wk, wvt, ones)


ATTN_GENS = 3
KEY_BLOCKS_PER_TILE = 2
HEADS_PER_STEP = 2
ATTN_UNROLL = 8
ATTN_LAG = 2
KEY_SLICE = 128
PV_DEPTH = 256


def _attn_body(qt_ref, k_ref, vt_ref, ot_ref, s_buf, p_buf, bm_buf, a_buf, m_ref, acc_ref, *, nblk):
    nq = qt_ref.shape[2]
    tk = p_buf.shape[1]
    per_head = nq * nblk
    nblocks = qt_ref.shape[1] * per_head
    nslices = tk // KEY_SLICE
    pv_every = PV_DEPTH // KEY_SLICE
    m_ref[...] = jnp.full(m_ref.shape, NEG_BIG, F32)
    acc_ref[...] = jnp.zeros(acc_ref.shape, F32)

    def step(t, phase, acc, j0=None, do_scores=True, do_softmax=True, do_pv=True):
        g_s, g_m, g_p = phase, (phase - ATTN_LAG) % ATTN_GENS, (phase - 2 * ATTN_LAG) % ATTN_GENS
        n_m, n_p = t - ATTN_LAG, t - 2 * ATTN_LAG
        j_s = t % nblk if j0 is None else j0
        j_m = n_m % nblk if j0 is None else (j0 - ATTN_LAG) % nblk
        j_p = n_p % nblk if j0 is None else (j0 - 2 * ATTN_LAG) % nblk
        if do_scores:
            start = j_s * tk if isinstance(j_s, int) else pl.multiple_of(j_s * tk, tk)
            h_s = t // per_head
            qt = qt_ref[0, h_s, (t // nblk) % nq]
        if do_softmax:
            if isinstance(j_m, int):
                m_old = jnp.full(m_ref.shape, NEG_BIG, F32) if j_m == 0 else m_ref[...]
            else:
                m_old = jnp.where(j_m == 0, NEG_BIG, m_ref[...])
            m_new = jnp.maximum(m_old, bm_buf[g_m])
            a_buf[g_m] = jnp.exp2(m_old - m_new)
            m_ref[...] = m_new
        if do_pv:
            acc = acc * a_buf[g_p]
            h_p = n_p // per_head
            v0 = h_p * V_ROWS if isinstance(h_p, int) else pl.multiple_of(h_p * V_ROWS, V_ROWS)
            vt = vt_ref.at[0, j_p, pl.ds(v0, V_ROWS), :]
        for c in range(nslices):
            rows = slice(c * KEY_SLICE, (c + 1) * KEY_SLICE)
            if do_pv and c % pv_every == pv_every - 1:
                deep = slice((c + 1 - pv_every) * KEY_SLICE, (c + 1) * KEY_SLICE)
                acc = acc + _dot(vt[:, deep], p_buf[g_p, deep, :])
            if do_softmax:
                p_buf[g_m, rows, :] = jnp.exp2(s_buf[g_m, rows, :] - m_new).astype(BF16)
            if do_scores and c == 0:
                s = _dot(k_ref[0, h_s, pl.ds(start, tk), :], qt)
                s_buf[g_s] = s
                bm_buf[g_s] = jnp.max(s, axis=0, keepdims=True)
        out = None
        if do_pv and not (isinstance(j_p, int) and j_p != nblk - 1):
            out = ((h_p, (n_p // nblk) % nq), acc)
        return acc, out

    def run(steps):
        outs = []
        for t, phase, j0, flags in steps:
            acc, out = step(t, phase, acc_ref[...], j0, **flags)
            acc_ref[...] = acc
            outs.append(out)
        for out in outs:
            if out is not None:
                ot_ref[0, out[0][0], out[0][1]] = out[1]

    def static_steps(lo, hi):
        run([(t, t % ATTN_GENS, t % nblk,
              dict(do_scores=t < nblocks, do_softmax=ATTN_LAG <= t < nblocks + ATTN_LAG, do_pv=t >= 2 * ATTN_LAG))
             for t in range(lo, hi)])

    fill = 2 * ATTN_LAG
    static_steps(0, fill)
    per_iter = ATTN_GENS * ATTN_UNROLL
    nloop = max(nblocks - fill, 0) // per_iter
    static_j = per_iter % nblk == 0

    def body(u, carry):
        t0 = fill + per_iter * u
        run([(t0 + i, (fill + i) % ATTN_GENS, (fill + i) % nblk if static_j else None, {}) for i in range(per_iter)])
        return carry

    lax.fori_loop(0, nloop, body, 0)
    static_steps(fill + per_iter * nloop, nblocks + fill)


def _attention(qt, k, vt, *, b, s, tq, tk):
    nq = s // tq
    nk = s // tk
    assert tk % PV_DEPTH == 0 and PV_DEPTH % KEY_SLICE == 0
    hps = HEADS_PER_STEP
    return pl.pallas_call(
        functools.partial(_attn_body, nblk=nk),
        out_shape=jax.ShapeDtypeStruct((b, MLA_HEADS, nq, V_ROWS, tq), F32),
        grid=(b, MLA_HEADS // hps),
        in_specs=[pl.BlockSpec((1, hps, nq, HEAD_LANES, tq), lambda bi, h: (bi, h, 0, 0, 0)),
                  pl.BlockSpec((1, hps, s, HEAD_LANES), lambda bi, h: (bi, h, 0, 0)),
                  pl.BlockSpec((1, nk, hps * V_ROWS, tk), lambda bi, h: (bi, 0, h, 0))],
        out_specs=pl.BlockSpec((1, hps, nq, V_ROWS, tq), lambda bi, h: (bi, h, 0, 0, 0)),
        scratch_shapes=[pltpu.VMEM((ATTN_GENS, tk, tq), F32), pltpu.VMEM((ATTN_GENS, tk, tq), BF16),
                        pltpu.VMEM((ATTN_GENS, 1, tq), F32), pltpu.VMEM((ATTN_GENS, 1, tq), F32),
                        pltpu.VMEM((1, tq), F32), pltpu.VMEM((V_ROWS, tq), F32)],
        compiler_params=_grid_params(2),
        name="attention",
    )(qt, k, vt)


HALO = 8


def _gdn_prep_body(prev_ref, cur_ref, next_ref, cw_ref, q_ref, k_ref, v_ref, buf_ref, *, tiles_per_seq):
    i = pl.program_id(0)
    tm = cur_ref.shape[0]
    first = (i % tiles_per_seq) == 0
    last = (i % tiles_per_seq) == tiles_per_seq - 1
    buf_ref[0:HALO, :] = jnp.where(first, 0.0, prev_ref[...])
    buf_ref[HALO:HALO + tm, :] = cur_ref[...]
    buf_ref[HALO + tm:2 * HALO + tm, :] = jnp.where(last, 0.0, next_ref[...])
    pad = GDN_CONV // 2
    for grp in range(3 * GDN_HEADS):
        lanes = slice(grp * GDN_D, (grp + 1) * GDN_D)
        acc = None
        for j in range(GDN_CONV):
            lo = HALO - pad + j
            term = buf_ref[lo:lo + tm, lanes] * cw_ref[j:j + 1, lanes]
            acc = term if acc is None else acc + term
        y = _silu(acc)
        if grp < 2 * GDN_HEADS:
            y = y * lax.rsqrt(jnp.sum(y * y, axis=-1, keepdims=True) + EPS)
        if grp < GDN_HEADS:
            q_ref[:, lanes] = y * GDN_D ** -0.5
        elif grp < 2 * GDN_HEADS:
            k_ref[:, slice(lanes.start - GDN_W, lanes.stop - GDN_W)] = y
        else:
            v_ref[:, slice(lanes.start - 2 * GDN_W, lanes.stop - 2 * GDN_W)] = y


def _gdn_prep(qkv, conv_w, *, s, tm):
    t, c = qkv.shape
    tiles_per_seq = s // tm
    hb = tm // HALO
    nh = t // HALO
    out = jax.ShapeDtypeStruct((t, GDN_W), F32)
    row = pl.BlockSpec((tm, GDN_W), lambda i: (i, 0))
    return pl.pallas_call(
        functools.partial(_gdn_prep_body, tiles_per_seq=tiles_per_seq),
        out_shape=(out, out, out),
        grid=(t // tm,),
        in_specs=[pl.BlockSpec((HALO, c), lambda i: (jnp.maximum(i * hb - 1, 0), 0)),
                  pl.BlockSpec((tm, c), lambda i: (i, 0)),
                  pl.BlockSpec((HALO, c), lambda i: (jnp.minimum((i + 1) * hb, nh - 1), 0)),
                  _const_spec(conv_w.shape)],
        out_specs=(row, row, row),
        scratch_shapes=[pltpu.VMEM((tm + 2 * HALO, c), F32)],
        compiler_params=_grid_params(1),
        name="gdn_prep",
    )(qkv, qkv, qkv, conv_w)


CH = GDN_CHUNK
HC = GDN_HEADS * CH
SCAN_CHUNKS = 2


def _split3(x):
    hi = x.astype(BF16)
    r = x - hi.astype(F32)
    mid = r.astype(BF16)
    lo = (r - mid.astype(F32)).astype(BF16)
    return hi, mid, lo


def _bcast_cols(cols, width):
    if width == 128:
        return jnp.concatenate([jnp.broadcast_to(c, (CH, 128)) for c in cols], axis=1)
    lane = lax.broadcasted_iota(jnp.int32, (CH, 128), 1)
    lo = lane < 64
    pair = lambda a, b: jnp.where(lo, jnp.broadcast_to(a, (CH, 128)), jnp.broadcast_to(b, (CH, 128)))
    return jnp.concatenate([pair(cols[0], cols[1]), pair(cols[2], cols[3])], axis=1)


def _dir_masks(d):
    r64 = lax.broadcasted_iota(jnp.int32, (CH, CH), 0)
    c64 = lax.broadcasted_iota(jnp.int32, (CH, CH), 1)
    r256 = lax.broadcasted_iota(jnp.int32, (CH, HC), 0)
    c256 = lax.broadcasted_iota(jnp.int32, (CH, HC), 1) & (CH - 1)
    if d == 0:
        tri, incl, strict, trit, last = (c64 <= r64), (c256 <= r256), (c256 < r256), (r256 <= c256), CH - 1
    else:
        tri, incl, strict, trit, last = (c64 >= r64), (c256 >= r256), (c256 > r256), (r256 >= c256), 0
    tri3 = jnp.concatenate([tri.astype(BF16)] * 3, axis=1)
    trit3 = jnp.concatenate([trit.astype(BF16)] * 3, axis=0)
    return dict(tri3=tri3, trit3=trit3, incl=incl, strict=strict, last=last)


def _pair_diag(a, b):
    z = jnp.zeros(a.shape, a.dtype)
    return jnp.concatenate([jnp.concatenate([a, z], axis=1), jnp.concatenate([z, b], axis=1)], axis=0)


def _gdn_scan_body(qf_ref, kf_ref, vf_ref, bgf_ref, gtf_ref, qb_ref, kb_ref, vb_ref, bgb_ref, gtb_ref,
                   of_ref, ob_ref, s_ref):
    @pl.when(pl.program_id(0) == 0)
    def _():
        s_ref[...] = jnp.zeros(s_ref.shape, F32)

    nb = qf_ref.shape[0]
    ncs = qf_ref.shape[1] // CH
    in_refs = ((qf_ref, kf_ref, vf_ref, bgf_ref, gtf_ref), (qb_ref, kb_ref, vb_ref, bgb_ref, gtb_ref))
    out_refs = (of_ref, ob_ref)
    masks = (_dir_masks(0), _dir_masks(1))
    groups = [(d, b, r if d == 0 else ncs - 1 - r) for r in range(ncs) for b in range(nb) for d in (0, 1)]
    per_rank = 2 * nb

    def rows(d, idx, b, c):
        return in_refs[d][idx][b, c * CH:(c + 1) * CH, :]
    heads = range(GDN_HEADS)
    pairs = range(GDN_HEADS // 2)

    r256 = lax.broadcasted_iota(jnp.int32, (CH, HC), 0)
    c256 = lax.broadcasted_iota(jnp.int32, (CH, HC), 1) & (CH - 1)
    eye = (c256 == r256).astype(F32)
    bd_mask = ((lax.broadcasted_iota(jnp.int32, (HC, HC), 0) >> 6)
               == (lax.broadcasted_iota(jnp.int32, (HC, HC), 1) >> 6)).astype(BF16)
    kbd_mask = ((lax.broadcasted_iota(jnp.int32, (HC, GDN_W), 0) >> 6)
                == (lax.broadcasted_iota(jnp.int32, (HC, GDN_W), 1) >> 7)).astype(BF16)
    r8 = lax.broadcasted_iota(jnp.int32, (2 * GDN_HEADS, HC), 0)
    h8 = lax.broadcasted_iota(jnp.int32, (2 * GDN_HEADS, HC), 1) >> 6

    st = []
    for d, b, c in groups:
        mk = masks[d]
        bg = rows(d, 3, b, c)
        gt = in_refs[d][4][b, c]
        cs = _dot(mk["tri3"], jnp.concatenate(_split3(bg), axis=0))
        cst = _dot(jnp.concatenate(_split3(gt), axis=1), mk["trit3"])
        st.append(dict(bg=bg, cs=cs, cst=cst))

    for g, (d, b, c) in enumerate(groups):
        mk, e = masks[d], st[g]
        crow = jnp.sum(jnp.where(r8 == d * GDN_HEADS + h8, e["cst"], 0.0), axis=0, keepdims=True)
        goff = 2 * GDN_HEADS + d * GDN_HEADS
        ccols = [e["cs"][:, goff + h:goff + h + 1] for h in heads]
        betas = [e["bg"][:, d * GDN_HEADS + h:d * GDN_HEADS + h + 1] for h in heads]
        e["ccols"] = ccols
        e["ccol512"] = _bcast_cols(ccols, GDN_D)
        e["beta512"] = _bcast_cols(betas, GDN_D)
        e["decay"] = jnp.exp(jnp.where(mk["incl"], _bcast_cols(ccols, CH) - crow, -jnp.inf))

    for g, (d, b, c) in enumerate(groups):
        mk, e = masks[d], st[g]
        q = rows(d, 0, b, c)
        k = rows(d, 1, b, c)
        kb = k * e["beta512"]
        kbd = jnp.concatenate([k.astype(BF16)] * GDN_HEADS, axis=0) * kbd_mask
        kq = lax.dot_general(jnp.concatenate([kb, q], axis=0).astype(BF16), kbd, NT_DIMS,
                             preferred_element_type=F32)
        neg_l = jnp.where(mk["strict"], -(kq[:CH] * e["decay"]), 0.0)
        e["intra"] = (kq[CH:] * e["decay"]).astype(BF16)
        e["p"] = eye + neg_l
        e["lm"] = neg_l

    for level in range(6):
        for e in st:
            w_bd = jnp.concatenate([e["lm"].astype(BF16)] * GDN_HEADS, axis=0) * bd_mask
            if level == 0:
                e["lm"] = _dot(e["lm"].astype(BF16), w_bd)
            elif level < 5:
                y = _dot(jnp.concatenate([e["p"], e["lm"]], axis=0).astype(BF16), w_bd)
                e["p"] = e["p"] + y[:CH]
                e["lm"] = y[CH:]
            else:
                e["p"] = e["p"] + _dot(e["p"].astype(BF16), w_bd)

    for g, (d, b, c) in enumerate(groups):
        e = st[g]
        k = rows(d, 1, b, c)
        v = rows(d, 2, b, c)
        tinv = e["p"].astype(BF16)
        vb = (v * e["beta512"]).astype(BF16)
        kbe = (k * e["beta512"] * jnp.exp(e["ccol512"])).astype(BF16)
        e["uw"] = []
        for pr in pairs:
            h0, h1 = 2 * pr, 2 * pr + 1
            rhs = lambda h: jnp.concatenate([vb[:, h * GDN_D:(h + 1) * GDN_D], kbe[:, h * GDN_D:(h + 1) * GDN_D]], axis=1)
            e["uw"].append(_dot(tinv[:, pr * GDN_D:(pr + 1) * GDN_D], _pair_diag(rhs(h0), rhs(h1))))

    for r in range(ncs):
        _gdn_state_stage(st, groups, r * per_rank, (r + 1) * per_rank, rows, out_refs, masks, s_ref)


def _gdn_state_stage(st, groups, lo, hi, rows, out_refs, masks, s_ref):
    pairs = range(GDN_HEADS // 2)
    for g in range(lo, hi):
        d, b, c = groups[g]
        e = st[g]
        q = rows(d, 0, b, c)
        qe = q * jnp.exp(e["ccol512"])
        e["ws"] = []
        for pr in pairs:
            lhs = []
            for h in (2 * pr, 2 * pr + 1):
                w = e["uw"][pr][:, (2 * (h % 2) + 1) * GDN_D:(2 * (h % 2) + 2) * GDN_D]
                lhs.append(jnp.concatenate([w, qe[:, h * GDN_D:(h + 1) * GDN_D]], axis=0))
            sidx = (b * 2 + d) * GDN_HEADS + 2 * pr
            sbd = _pair_diag(s_ref[sidx].astype(BF16), s_ref[sidx + 1].astype(BF16))
            e["ws"].append(_dot(jnp.concatenate(lhs, axis=1).astype(BF16), sbd))

    for g in range(lo, hi):
        d, b, c = groups[g]
        e = st[g]
        k = rows(d, 1, b, c)
        last = masks[d]["last"]
        for pr in pairs:
            vnew = []
            for h in (2 * pr, 2 * pr + 1):
                u = e["uw"][pr][:, 2 * (h % 2) * GDN_D:(2 * (h % 2) + 1) * GDN_D]
                vnew.append((u - e["ws"][pr][:CH, (h % 2) * GDN_D:(h % 2 + 1) * GDN_D]).astype(BF16))
            glast = [e["ccols"][h][last:last + 1, :] for h in (2 * pr, 2 * pr + 1)]
            kd = [k[:, h * GDN_D:(h + 1) * GDN_D] * jnp.exp(gl - e["ccol512"][:, h * GDN_D:(h + 1) * GDN_D])
                  for h, gl in zip((2 * pr, 2 * pr + 1), glast)]
            kdt = jnp.concatenate(kd, axis=0).T.astype(BF16)
            lhs = jnp.concatenate([e["intra"][:, pr * GDN_D:(pr + 1) * GDN_D], kdt], axis=0)
            prod = _dot(lhs, _pair_diag(vnew[0], vnew[1]))
            out_refs[d][b, c * CH:(c + 1) * CH, 2 * pr * GDN_D:(2 * pr + 2) * GDN_D] = (
                e["ws"][pr][CH:, :] + prod[:CH]).astype(BF16)
            for i, h in enumerate((2 * pr, 2 * pr + 1)):
                sidx = (b * 2 + d) * GDN_HEADS + h
                s_ref[sidx] = s_ref[sidx] * jnp.exp(glast[i]) + prod[CH:, i * GDN_D:(i + 1) * GDN_D]


def _gdn_scan(q, k, v, bg, gt, *, b, s):
    ncs = SCAN_CHUNKS if (s // CH) % SCAN_CHUNKS == 0 else 1
    n = s // (CH * ncs)
    fwd = lambda i: (0, i, 0)
    bwd = lambda i: (0, n - 1 - i, 0)
    fwd_t = lambda i: (0, i, 0, 0)
    bwd_t = lambda i: (0, n - 1 - i, 0, 0)
    wide = lambda im: pl.BlockSpec((b, ncs * CH, GDN_W), im)
    specs = lambda im, imt: [wide(im), wide(im), wide(im), pl.BlockSpec((b, ncs * CH, N_BA), im),
                             pl.BlockSpec((b, ncs, 2 * GDN_HEADS, CH), imt)]
    out = jax.ShapeDtypeStruct((b, s, GDN_W), BF16)
    return pl.pallas_call(
        _gdn_scan_body,
        out_shape=(out, out),
        grid=(n,),
        in_specs=specs(fwd, fwd_t) + specs(bwd, bwd_t),
        out_specs=(wide(fwd), wide(bwd)),
        scratch_shapes=[pltpu.VMEM((b * 2 * GDN_HEADS, GDN_D, GDN_D), F32)],
        compiler_params=_grid_params(1),
        name="gdn_scan",
    )(q, k, v, bg, gt, q, k, v, bg, gt)


def _merge_body(x_ref, of_ref, ob_ref, sz_ref, sg_ref, ot_ref, gnw_ref, wgp_ref, wmp_ref, wo_ref, o_ref):
    o = of_ref[...].astype(F32) + ob_ref[...].astype(F32)
    gnw = gnw_ref[...]
    heads = []
    for h in range(GDN_HEADS):
        oh = o[:, h * GDN_D:(h + 1) * GDN_D]
        heads.append(_rms(oh, gnw))
    on = jnp.concatenate(heads, axis=1) * sz_ref[...]
    ya = _dot(on.astype(BF16), wgp_ref[...])
    acc_t = ot_ref[0, :, 0]
    ot = (acc_t[:, :MLA_V, :] / acc_t[:, MLA_V:MLA_V + 1, :]).reshape(MLA_HEADS * MLA_V, -1)
    yb = _dot(ot.T.astype(BF16), wmp_ref[...])
    d = ya.shape[1]
    y = sg_ref[:, :d] * ya + sg_ref[:, d:] * yb
    o_ref[...] = x_ref[...] + _dot(y.astype(BF16), wo_ref[...])


def _merge(x, of, ob, sz, sg, ot, gnw, wgp, wmp, wo, *, b, s, tm):
    t, d = x.shape
    nsteps = s // tm
    row = lambda n: pl.BlockSpec((tm, n), lambda bi, i: (bi * nsteps + i, 0))
    return pl.pallas_call(
        _merge_body,
        out_shape=jax.ShapeDtypeStruct((t, d), F32),
        grid=(b, nsteps),
        in_specs=[row(d), row(GDN_W), row(GDN_W), row(GDN_W), row(2 * d),
                  pl.BlockSpec((1, MLA_HEADS, 1, V_ROWS, tm), lambda bi, i: (bi, 0, i, 0, 0)),
                  _const_spec((1, GDN_D)), _const_spec(wgp.shape), _const_spec(wmp.shape), _const_spec(wo.shape)],
        out_specs=row(d),
        compiler_params=_grid_params(2),
        name="merge",
    )(x, of, ob, sz, sg, ot, gnw, wgp, wmp, wo)


def _pack_w_in(w):
    d = w.shape[0]
    zeros = lambda n: jnp.zeros((d, n), w.dtype)
    o = 4 * GDN_W
    ba = w[:, o:o + N_BA]
    o += N_BA
    cq = w[:, o:o + MLA_Q_LORA]
    o += MLA_Q_LORA
    ckv = w[:, o:o + MLA_KV_LORA]
    o += MLA_KV_LORA
    kr = w[:, o:o + MLA_ROPE]
    o += MLA_ROPE
    gates = w[:, o:]
    half = MLA_ROPE // 2
    tail = zeros(HEAD_LANES - MLA_NOPE - MLA_ROPE)
    kr_main = jnp.concatenate([zeros(MLA_NOPE), kr, tail], axis=1)
    kr_swap = jnp.concatenate([zeros(MLA_NOPE), kr[:, half:], kr[:, :half], tail], axis=1)
    packed = jnp.concatenate([w[:, :4 * GDN_W], ba, zeros(128 - N_BA), cq, ckv, kr_main, kr_swap, gates], axis=1)
    return packed.astype(BF16), ba.T.astype(BF16)


def _pack_w_uq(w):
    r = w.shape[0]
    qk = MLA_NOPE + MLA_ROPE
    half = MLA_ROPE // 2
    tail = jnp.zeros((r, HEAD_LANES - qk), w.dtype)
    znope = jnp.zeros((r, MLA_NOPE), w.dtype)
    main, swap = [], []
    for h in range(MLA_HEADS):
        nope = w[:, h * qk:h * qk + MLA_NOPE]
        rope = w[:, h * qk + MLA_NOPE:(h + 1) * qk]
        main += [nope, rope, tail]
        swap += [znope, rope[:, half:], rope[:, :half], tail]
    return jnp.concatenate(main, axis=1).T.astype(BF16), jnp.concatenate(swap, axis=1).T.astype(BF16)


def _pack_w_ukv(w):
    r = w.shape[0]
    hw = MLA_NOPE + MLA_V
    ks, vs = [], []
    for h in range(MLA_HEADS):
        ks += [w[:, h * hw:h * hw + MLA_NOPE], jnp.zeros((r, HEAD_LANES - MLA_NOPE), w.dtype)]
        vs += [w[:, h * hw + MLA_NOPE:(h + 1) * hw], jnp.zeros((r, V_ROWS - MLA_V), w.dtype)]
    return jnp.concatenate(ks, axis=1).astype(BF16), jnp.concatenate(vs, axis=1).T.astype(BF16)


def _ones_rows():
    idx = jnp.arange(MLA_HEADS * V_ROWS) % V_ROWS
    return (idx == MLA_V).astype(F32)[:, None]


def _lane_pad(v, lo, width):
    return jnp.zeros((1, width), v.dtype).at[0, lo:lo + v.shape[0]].set(v)


def kernel(x, positions, norm_ffn1, ffn1_w_gate, ffn1_w_up, ffn1_w_down, norm_mix, w_in, gdn_conv, gdn_A_log,
           gdn_dt_bias, gdn_norm, gdn_proj, mla_q_norm, mla_w_uq, mla_kv_norm, mla_w_ukv, mla_proj, w_out,
           norm_ffn2, ffn2_w_gate, ffn2_w_up, ffn2_w_down, final_norm):
    b, s, d = x.shape
    t = b * s
    depth = w_in.shape[0]
    tm = 512 if s % 2048 == 0 else 256
    ffn_tm = 512
    tq = tm
    tk = KEY_BLOCKS_PER_TILE * tm

    cos, sin, cost, sint = _rope_tables(positions, tm)
    ones = _ones_rows()
    xf = x.reshape(t, d)
    row = lambda v: v.reshape(1, -1)
    for l in range(depth):
        xf = _ffn(xf, row(norm_ffn1[l]), ffn1_w_gate[l].astype(BF16), ffn1_w_up[l].astype(BF16),
                  ffn1_w_down[l].astype(BF16), row(final_norm), final_norm=False, tm=ffn_tm)

        w_packed, w_bat = _pack_w_in(w_in[l])
        alog = gdn_A_log[l].reshape(-1)
        dtb = gdn_dt_bias[l].reshape(-1)
        qkv, sz, bg, gt, cqn, ckvn, kr, sg = _inproj(
            xf, row(norm_mix[l]), w_packed, w_bat, _lane_pad(alog, 2 * GDN_HEADS, 128),
            _lane_pad(dtb, 2 * GDN_HEADS, 128), alog[:, None], dtb[:, None], row(mla_q_norm[l]),
            row(mla_kv_norm[l]), cos, sin, tm=tm)

        qn, kn, vv = _gdn_prep(qkv, gdn_conv[l], s=s, tm=tm)
        seq = lambda a: a.reshape(b, s, a.shape[-1])
        gt_chunks = gt.reshape(2 * GDN_HEADS, b, s // CH, CH).transpose(1, 2, 0, 3)
        of, ob = _gdn_scan(seq(qn), seq(kn), seq(vv), seq(bg), gt_chunks, b=b, s=s)
        of, ob = of.reshape(t, GDN_W), ob.reshape(t, GDN_W)

        wqm, wqp = _pack_w_uq(mla_w_uq[l])
        wk, wvt = _pack_w_ukv(mla_w_ukv[l])
        qt, kk, vt = _mla_prep(cqn, ckvn, kr, cost, sint, wqm, wqp, wk, wvt, ones, b=b, s=s, tq=tq, tk=tk)
        ot = _attention(qt, kk, vt, b=b, s=s, tq=tq, tk=tk)

        xf = _merge(xf, of, ob, sz, sg, ot, row(gdn_norm[l]), gdn_proj[l].astype(BF16), mla_proj[l].astype(BF16),
                    w_out[l].astype(BF16), b=b, s=s, tm=tm)

        xf = _ffn(xf, row(norm_ffn2[l]), ffn2_w_gate[l].astype(BF16), ffn2_w_up[l].astype(BF16),
                  ffn2_w_down[l].astype(BF16), row(final_norm), final_norm=(l == depth - 1), tm=ffn_tm)
    return xf.reshape(b, s, d)
```

```python
import functools

import jax
import jax.numpy as jnp
from jax import lax
from jax.experimental import pallas as pl
from jax.experimental.pallas import tpu as pltpu

F32 = jnp.float32
BF16 = jnp.bfloat16

EPS = 1e-6
RES_HALF = 0.5
GDN_HEADS = 4
GDN_D = 128
GDN_CONV = 5
GDN_CHUNK = 64
MLA_HEADS = 8
MLA_NOPE = 64
MLA_ROPE = 32
MLA_V = 64
MLA_Q_LORA = 384
MLA_KV_LORA = 256
ROPE_THETA = 10000.0
HEAD_LANES = 128
V_ROWS = 80
LOG2E = 1.4426950408889634
NEG_BIG = -1e30

VMEM_LIMIT_BYTES = 56 * 1024 * 1024

NT_DIMS = (((1,), (1,)), ((), ()))


def _grid_params(n, flags=None):
    return pltpu.CompilerParams(dimension_semantics=("arbitrary",) * n, vmem_limit_bytes=VMEM_LIMIT_BYTES,
                                flags=flags)


def _const_spec(shape):
    nd = len(shape)
    return pl.BlockSpec(shape, lambda *_: (0,) * nd, pipeline_mode=pl.Buffered(1))


def _rms(x, w):
    return x * lax.rsqrt(jnp.mean(x * x, axis=-1, keepdims=True) + EPS) * w


def _silu(x):
    return x * jax.nn.sigmoid(x)


def _softplus(x):
    return jnp.maximum(x, 0.0) + jnp.log1p(jnp.exp(-jnp.abs(x)))


def _dot(a, b):
    return jnp.dot(a, b, preferred_element_type=F32)


def _ffn_body(x_ref, nw_ref, wg_ref, wu_ref, wd_ref, fw_ref, o_ref, *, final_norm):
    x = x_ref[...]
    hb = _rms(x, nw_ref[...]).astype(BF16)
    g = _dot(hb, wg_ref[...])
    u = _dot(hb, wu_ref[...])
    a = (_silu(g) * u).astype(BF16)
    y = x + RES_HALF * _dot(a, wd_ref[...])
    if final_norm:
        y = _rms(y, fw_ref[...])
    o_ref[...] = y


def _ffn(x, nw, wg, wu, wd, fw, *, final_norm, tm):
    t, d = x.shape
    ff = wg.shape[1]
    row = pl.BlockSpec((tm, d), lambda i: (i, 0))
    return pl.pallas_call(
        functools.partial(_ffn_body, final_norm=final_norm),
        out_shape=jax.ShapeDtypeStruct((t, d), F32),
        grid=(t // tm,),
        in_specs=[row, _const_spec((1, d)), _const_spec((d, ff)), _const_spec((d, ff)),
                  _const_spec((ff, d)), _const_spec((1, d))],
        out_specs=row,
        compiler_params=_grid_params(1),
        name="ffn",
    )(x, nw, wg, wu, wd, fw)


def _rope_body(post_ref, fcol_ref, cos_ref, sin_ref, cost_ref, sint_ref):
    ang = fcol_ref[...] * post_ref[0].astype(F32)
    c = jnp.cos(ang)
    s = jnp.sin(ang)
    tm = ang.shape[1]
    pad = HEAD_LANES - MLA_NOPE - MLA_ROPE
    cost = jnp.concatenate([jnp.ones((MLA_NOPE, tm), F32), c, c, jnp.ones((pad, tm), F32)], axis=0)
    sint = jnp.concatenate([jnp.zeros((MLA_NOPE, tm), F32), -s, s, jnp.zeros((pad, tm), F32)], axis=0)
    cost_ref[0] = cost
    sint_ref[0] = sint
    cos_ref[...] = cost.T
    sin_ref[...] = sint.T


def _rope_tables(positions, tm):
    b, s = positions.shape
    t = b * s
    inv_freq = jnp.power(ROPE_THETA, -jnp.arange(0, MLA_ROPE, 2, dtype=F32) / MLA_ROPE)
    nsteps = s // tm
    return pl.pallas_call(
        _rope_body,
        out_shape=(jax.ShapeDtypeStruct((t, HEAD_LANES), F32), jax.ShapeDtypeStruct((t, HEAD_LANES), F32),
                   jax.ShapeDtypeStruct((b, HEAD_LANES, s), F32), jax.ShapeDtypeStruct((b, HEAD_LANES, s), F32)),
        grid=(b, nsteps),
        in_specs=[pl.BlockSpec((1, 1, tm), lambda bi, i: (bi, 0, i)), _const_spec((MLA_ROPE // 2, 1))],
        out_specs=(pl.BlockSpec((tm, HEAD_LANES), lambda bi, i: (bi * nsteps + i, 0)),
                   pl.BlockSpec((tm, HEAD_LANES), lambda bi, i: (bi * nsteps + i, 0)),
                   pl.BlockSpec((1, HEAD_LANES, tm), lambda bi, i: (bi, 0, i)),
                   pl.BlockSpec((1, HEAD_LANES, tm), lambda bi, i: (bi, 0, i))),
        compiler_params=_grid_params(2),
        name="rope_tables",
    )(positions.reshape(b, 1, s), inv_freq[:, None])


GDN_W = GDN_HEADS * GDN_D
SEG_QKV = (0, 3 * GDN_W)
SEG_Z = (SEG_QKV[1], SEG_QKV[1] + GDN_W)
SEG_BA = (SEG_Z[1], SEG_Z[1] + 128)
SEG_CQ = (SEG_BA[1], SEG_BA[1] + MLA_Q_LORA)
SEG_CKV = (SEG_CQ[1], SEG_CQ[1] + MLA_KV_LORA)
SEG_KRM = (SEG_CKV[1], SEG_CKV[1] + HEAD_LANES)
SEG_KRP = (SEG_KRM[1], SEG_KRM[1] + HEAD_LANES)
SEG_GATE = (SEG_KRP[1], SEG_KRP[1] + 2048)
N_BA = 4 * GDN_HEADS


def _inproj_body(x_ref, nw_ref, w_ref, wbat_ref, alog_ref, dtb_ref, alogt_ref, dtbt_ref, qnw_ref, kvnw_ref,
                 cos_ref, sin_ref, qkv_ref, sz_ref, bg_ref, gt_ref, cqn_ref, ckvn_ref, kr_ref, sg_ref):
    hb = _rms(x_ref[...], nw_ref[...]).astype(BF16)

    def seg(bounds):
        return _dot(hb, w_ref[:, bounds[0]:bounds[1]])

    qkv_ref[...] = seg(SEG_QKV)
    sz_ref[...] = _silu(seg(SEG_Z)).astype(BF16)
    ba = seg(SEG_BA)
    lane = lax.broadcasted_iota(jnp.int32, ba.shape, 1)
    decay = -jnp.exp(alog_ref[...]) * _softplus(ba + dtb_ref[...])
    bg_ref[...] = jnp.where(lane < 2 * GDN_HEADS, jax.nn.sigmoid(ba), decay)[:, :N_BA]
    bat = lax.dot_general(wbat_ref[...], hb, NT_DIMS, preferred_element_type=F32)
    at = bat[2 * GDN_HEADS:, :]
    gt_ref[...] = -jnp.exp(alogt_ref[...]) * _softplus(at + dtbt_ref[...])
    cqn_ref[...] = _rms(seg(SEG_CQ), qnw_ref[...]).astype(BF16)
    ckvn_ref[...] = _rms(seg(SEG_CKV), kvnw_ref[...]).astype(BF16)
    kr_ref[...] = seg(SEG_KRM) * cos_ref[...] + seg(SEG_KRP) * sin_ref[...]
    sg_ref[...] = jax.nn.sigmoid(seg(SEG_GATE)).astype(BF16)


def _inproj(x, nw, w, wbat, alog, dtb, alogt, dtbt, qnw, kvnw, cos, sin, *, tm):
    t, d = x.shape
    row = lambda n: pl.BlockSpec((tm, n), lambda i: (i, 0))
    out_shape = (
        jax.ShapeDtypeStruct((t, 3 * GDN_W), F32),
        jax.ShapeDtypeStruct((t, GDN_W), BF16),
        jax.ShapeDtypeStruct((t, N_BA), F32),
        jax.ShapeDtypeStruct((2 * GDN_HEADS, t), F32),
        jax.ShapeDtypeStruct((t, MLA_Q_LORA), BF16),
        jax.ShapeDtypeStruct((t, MLA_KV_LORA), BF16),
        jax.ShapeDtypeStruct((t, HEAD_LANES), F32),
        jax.ShapeDtypeStruct((t, 2048), BF16),
    )
    out_specs = (row(3 * GDN_W), row(GDN_W), row(N_BA), pl.BlockSpec((2 * GDN_HEADS, tm), lambda i: (0, i)),
                 row(MLA_Q_LORA), row(MLA_KV_LORA), row(HEAD_LANES), row(2048))
    return pl.pallas_call(
        _inproj_body,
        out_shape=out_shape,
        grid=(t // tm,),
        in_specs=[row(d), _const_spec((1, d)), _const_spec(w.shape), _const_spec(wbat.shape),
                  _const_spec((1, 128)), _const_spec((1, 128)),
                  _const_spec((2 * GDN_HEADS, 1)), _const_spec((2 * GDN_HEADS, 1)),
                  _const_spec((1, MLA_Q_LORA)), _const_spec((1, MLA_KV_LORA)),
                  row(HEAD_LANES), row(HEAD_LANES)],
        out_specs=out_specs,
        compiler_params=_grid_params(1),
        name="inproj",
    )(x, nw, w, wbat, alog, dtb, alogt, dtbt, qnw, kvnw, cos, sin)


def _mla_prep_body(cqn_ref, ckvn_ref, kr_ref, cost_ref, sint_ref, wqm_ref, wqp_ref, wk_ref, wvt_ref, ones_ref,
                   qt_ref, k_ref, vt_ref):
    cqn = cqn_ref[...]
    qm = lax.dot_general(wqm_ref[...], cqn, NT_DIMS, preferred_element_type=F32)
    qp = lax.dot_general(wqp_ref[...], cqn, NT_DIMS, preferred_element_type=F32)
    qscale = (MLA_NOPE + MLA_ROPE) ** -0.5 * LOG2E
    cost = cost_ref[0] * qscale
    sint = sint_ref[0] * qscale
    ckvn = ckvn_ref[...]
    km = _dot(ckvn, wk_ref[...])
    kr = kr_ref[...]
    for h in range(MLA_HEADS):
        grp = slice(h * HEAD_LANES, (h + 1) * HEAD_LANES)
        qt_ref[0, h, 0] = (qm[grp, :] * cost + qp[grp, :] * sint).astype(BF16)
        k_ref[0, h] = (km[:, grp] + kr).astype(BF16)
    vt = lax.dot_general(wvt_ref[...], ckvn, NT_DIMS, preferred_element_type=F32)
    vt_ref[0, 0] = (vt + ones_ref[...]).astype(BF16)


def _mla_prep(cqn, ckvn, kr, cost, sint, wqm, wqp, wk, wvt, ones, *, b, s, tq, tk):
    nsteps = s // tq
    per_key_block = tk // tq
    vr = MLA_HEADS * V_ROWS
    row = lambda n: pl.BlockSpec((tq, n), lambda bi, i: (bi * nsteps + i, 0))
    tr = pl.BlockSpec((1, HEAD_LANES, tq), lambda bi, i: (bi, 0, i))
    return pl.pallas_call(
        _mla_prep_body,
        out_shape=(jax.ShapeDtypeStruct((b, MLA_HEADS, nsteps, HEAD_LANES, tq), BF16),
                   jax.ShapeDtypeStruct((b, MLA_HEADS, s, HEAD_LANES), BF16),
                   jax.ShapeDtypeStruct((b, s // tk, vr, tk), BF16)),
        grid=(b, nsteps),
        in_specs=[row(MLA_Q_LORA), row(MLA_KV_LORA), row(HEAD_LANES), tr, tr,
                  _const_spec(wqm.shape), _const_spec(wqp.shape), _const_spec(wk.shape), _const_spec(wvt.shape),
                  _const_spec((vr, 1))],
        out_specs=(pl.BlockSpec((1, MLA_HEADS, 1, HEAD_LANES, tq), lambda bi, i: (bi, 0, i, 0, 0)),
                   pl.BlockSpec((1, MLA_HEADS, tq, HEAD_LANES), lambda bi, i: (bi, 0, i, 0)),
                   pl.BlockSpec((1, 1, vr, tq), lambda bi, i: (bi, i // per_key_block, 0, i % per_key_block))),
        compiler_params=_grid_params(2),
        name="mla_prep",
    )(cqn, ckvn, kr, cost, sint, wqm, wqp, wk, wvt, ones)


ATTN_GENS = 3
KEY_BLOCKS_PER_TILE = 2
HEADS_PER_STEP = 2
ATTN_UNROLL = 8
ATTN_LAG = 2
KEY_SLICE = 128
PV_DEPTH = 256


def _attn_body(qt_ref, k_ref, vt_ref, ot_ref, s_buf, p_buf, bm_buf, a_buf, m_ref, acc_ref, *, nblk):
    nq = qt_ref.shape[2]
    tk = p_buf.shape[1]
    per_head = nq * nblk
    nblocks = qt_ref.shape[1] * per_head
    nslices = tk // KEY_SLICE
    pv_every = PV_DEPTH // KEY_SLICE
    m_ref[...] = jnp.full(m_ref.shape, NEG_BIG, F32)
    acc_ref[...] = jnp.zeros(acc_ref.shape, F32)

    def step(t, phase, acc, j0=None, do_scores=True, do_softmax=True, do_pv=True):
        g_s, g_m, g_p = phase, (phase - ATTN_LAG) % ATTN_GENS, (phase - 2 * ATTN_LAG) % ATTN_GENS
        n_m, n_p = t - ATTN_LAG, t - 2 * ATTN_LAG
        j_s = t % nblk if j0 is None else j0
        j_m = n_m % nblk if j0 is None else (j0 - ATTN_LAG) % nblk
        j_p = n_p % nblk if j0 is None else (j0 - 2 * ATTN_LAG) % nblk
        if do_scores:
            start = j_s * tk if isinstance(j_s, int) else pl.multiple_of(j_s * tk, tk)
            h_s = t // per_head
            qt = qt_ref[0, h_s, (t // nblk) % nq]
        if do_softmax:
            if isinstance(j_m, int):
                m_old = jnp.full(m_ref.shape, NEG_BIG, F32) if j_m == 0 else m_ref[...]
            else:
                m_old = jnp.where(j_m == 0, NEG_BIG, m_ref[...])
            m_new = jnp.maximum(m_old, bm_buf[g_m])
            a_buf[g_m] = jnp.exp2(m_old - m_new)
            m_ref[...] = m_new
        if do_pv:
            acc = acc * a_buf[g_p]
            h_p = n_p // per_head
            v0 = h_p * V_ROWS if isinstance(h_p, int) else pl.multiple_of(h_p * V_ROWS, V_ROWS)
            vt = vt_ref.at[0, j_p, pl.ds(v0, V_ROWS), :]
        for c in range(nslices):
            rows = slice(c * KEY_SLICE, (c + 1) * KEY_SLICE)
            if do_pv and c % pv_every == pv_every - 1:
                deep = slice((c + 1 - pv_every) * KEY_SLICE, (c + 1) * KEY_SLICE)
                acc = acc + _dot(vt[:, deep], p_buf[g_p, deep, :])
            if do_softmax:
                p_buf[g_m, rows, :] = jnp.exp2(s_buf[g_m, rows, :] - m_new).astype(BF16)
            if do_scores and c == 0:
                s = _dot(k_ref[0, h_s, pl.ds(start, tk), :], qt)
                s_buf[g_s] = s
                bm_buf[g_s] = jnp.max(s, axis=0, keepdims=True)
        out = None
        if do_pv and not (isinstance(j_p, int) and j_p != nblk - 1):
            out = ((h_p, (n_p // nblk) % nq), acc)
        return acc, out

    def run(steps):
        outs = []
        for t, phase, j0, flags in steps:
            acc, out = step(t, phase, acc_ref[...], j0, **flags)
            acc_ref[...] = acc
            outs.append(out)
        for out in outs:
            if out is not None:
                ot_ref[0, out[0][0], out[0][1]] = out[1].astype(BF16)

    def static_steps(lo, hi):
        run([(t, t % ATTN_GENS, t % nblk,
              dict(do_scores=t < nblocks, do_softmax=ATTN_LAG <= t < nblocks + ATTN_LAG, do_pv=t >= 2 * ATTN_LAG))
             for t in range(lo, hi)])

    fill = 2 * ATTN_LAG
    static_steps(0, fill)
    per_iter = ATTN_GENS * ATTN_UNROLL
    nloop = max(nblocks - fill, 0) // per_iter
    static_j = per_iter % nblk == 0

    def body(u, carry):
        t0 = fill + per_iter * u
        run([(t0 + i, (fill + i) % ATTN_GENS, (fill + i) % nblk if static_j else None, {}) for i in range(per_iter)])
        return carry

    lax.fori_loop(0, nloop, body, 0)
    static_steps(fill + per_iter * nloop, nblocks + fill)


def _attention(qt, k, vt, *, b, s, tq, tk):
    nq = s // tq
    nk = s // tk
    assert tk % PV_DEPTH == 0 and PV_DEPTH % KEY_SLICE == 0
    hps = HEADS_PER_STEP
    return pl.pallas_call(
        functools.partial(_attn_body, nblk=nk),
        out_shape=jax.ShapeDtypeStruct((b, MLA_HEADS, nq, V_ROWS, tq), BF16),
        grid=(b, MLA_HEADS // hps),
        in_specs=[pl.BlockSpec((1, hps, nq, HEAD_LANES, tq), lambda bi, h: (bi, h, 0, 0, 0)),
                  pl.BlockSpec((1, hps, s, HEAD_LANES), lambda bi, h: (bi, h, 0, 0)),
                  pl.BlockSpec((1, nk, hps * V_ROWS, tk), lambda bi, h: (bi, 0, h, 0))],
        out_specs=pl.BlockSpec((1, hps, nq, V_ROWS, tq), lambda bi, h: (bi, h, 0, 0, 0)),
        scratch_shapes=[pltpu.VMEM((ATTN_GENS, tk, tq), F32), pltpu.VMEM((ATTN_GENS, tk, tq), BF16),
                        pltpu.VMEM((ATTN_GENS, 1, tq), F32), pltpu.VMEM((ATTN_GENS, 1, tq), F32),
                        pltpu.VMEM((1, tq), F32), pltpu.VMEM((V_ROWS, tq), F32)],
        compiler_params=_grid_params(2),
        name="attention",
    )(qt, k, vt)


HALO = 8


def _gdn_prep_body(prev_ref, cur_ref, next_ref, cw_ref, q_ref, k_ref, v_ref, buf_ref, *, tiles_per_seq):
    i = pl.program_id(0)
    tm = cur_ref.shape[0]
    first = (i % tiles_per_seq) == 0
    last = (i % tiles_per_seq) == tiles_per_seq - 1
    buf_ref[0:HALO, :] = jnp.where(first, 0.0, prev_ref[...])
    buf_ref[HALO:HALO + tm, :] = cur_ref[...]
    buf_ref[HALO + tm:2 * HALO + tm, :] = jnp.where(last, 0.0, next_ref[...])
    pad = GDN_CONV // 2
    for grp in range(3 * GDN_HEADS):
        lanes = slice(grp * GDN_D, (grp + 1) * GDN_D)
        acc = None
        for j in range(GDN_CONV):
            lo = HALO - pad + j
            term = buf_ref[lo:lo + tm, lanes] * cw_ref[j:j + 1, lanes]
            acc = term if acc is None else acc + term
        y = _silu(acc)
        if grp < 2 * GDN_HEADS:
            y = y * lax.rsqrt(jnp.sum(y * y, axis=-1, keepdims=True) + EPS)
        if grp < GDN_HEADS:
            q_ref[:, lanes] = y * GDN_D ** -0.5
        elif grp < 2 * GDN_HEADS:
            k_ref[:, slice(lanes.start - GDN_W, lanes.stop - GDN_W)] = y
        else:
            v_ref[:, slice(lanes.start - 2 * GDN_W, lanes.stop - 2 * GDN_W)] = y


def _gdn_prep(qkv, conv_w, *, s, tm):
    t, c = qkv.shape
    tiles_per_seq = s // tm
    hb = tm // HALO
    nh = t // HALO
    out = jax.ShapeDtypeStruct((t, GDN_W), F32)
    row = pl.BlockSpec((tm, GDN_W), lambda i: (i, 0))
    return pl.pallas_call(
        functools.partial(_gdn_prep_body, tiles_per_seq=tiles_per_seq),
        out_shape=(out, out, out),
        grid=(t // tm,),
        in_specs=[pl.BlockSpec((HALO, c), lambda i: (jnp.maximum(i * hb - 1, 0), 0)),
                  pl.BlockSpec((tm, c), lambda i: (i, 0)),
                  pl.BlockSpec((HALO, c), lambda i: (jnp.minimum((i + 1) * hb, nh - 1), 0)),
                  _const_spec(conv_w.shape)],
        out_specs=(row, row, row),
        scratch_shapes=[pltpu.VMEM((tm + 2 * HALO, c), F32)],
        compiler_params=_grid_params(1),
        name="gdn_prep",
    )(qkv, qkv, qkv, conv_w)


CH = GDN_CHUNK
HC = GDN_HEADS * CH
SCAN_CHUNKS = 2


def _split3(x):
    hi = x.astype(BF16)
    r = x - hi.astype(F32)
    mid = r.astype(BF16)
    lo = (r - mid.astype(F32)).astype(BF16)
    return hi, mid, lo


def _bcast_cols(cols, width):
    if width == 128:
        return jnp.concatenate([jnp.broadcast_to(c, (CH, 128)) for c in cols], axis=1)
    lane = lax.broadcasted_iota(jnp.int32, (CH, 128), 1)
    lo = lane < 64
    pair = lambda a, b: jnp.where(lo, jnp.broadcast_to(a, (CH, 128)), jnp.broadcast_to(b, (CH, 128)))
    return jnp.concatenate([pair(cols[0], cols[1]), pair(cols[2], cols[3])], axis=1)


def _dir_masks(d):
    r64 = lax.broadcasted_iota(jnp.int32, (CH, CH), 0)
    c64 = lax.broadcasted_iota(jnp.int32, (CH, CH), 1)
    r256 = lax.broadcasted_iota(jnp.int32, (CH, HC), 0)
    c256 = lax.broadcasted_iota(jnp.int32, (CH, HC), 1) & (CH - 1)
    if d == 0:
        tri, incl, strict, trit, last = (c64 <= r64), (c256 <= r256), (c256 < r256), (r256 <= c256), CH - 1
    else:
        tri, incl, strict, trit, last = (c64 >= r64), (c256 >= r256), (c256 > r256), (r256 >= c256), 0
    tri3 = jnp.concatenate([tri.astype(BF16)] * 3, axis=1)
    trit3 = jnp.concatenate([trit.astype(BF16)] * 3, axis=0)
    return dict(tri3=tri3, trit3=trit3, incl=incl, strict=strict, last=last)


def _pair_diag(a, b):
    z = jnp.zeros(a.shape, a.dtype)
    return jnp.concatenate([jnp.concatenate([a, z], axis=1), jnp.concatenate([z, b], axis=1)], axis=0)


def _gdn_scan_body(qf_ref, kf_ref, vf_ref, bgf_ref, gtf_ref, qb_ref, kb_ref, vb_ref, bgb_ref, gtb_ref,
                   of_ref, ob_ref, s_ref):
    @pl.when(pl.program_id(0) == 0)
    def _():
        s_ref[...] = jnp.zeros(s_ref.shape, F32)

    nb = qf_ref.shape[0]
    ncs = qf_ref.shape[1] // CH
    in_refs = ((qf_ref, kf_ref, vf_ref, bgf_ref, gtf_ref), (qb_ref, kb_ref, vb_ref, bgb_ref, gtb_ref))
    out_refs = (of_ref, ob_ref)
    masks = (_dir_masks(0), _dir_masks(1))
    groups = [(d, b, r if d == 0 else ncs - 1 - r) for r in range(ncs) for b in range(nb) for d in (0, 1)]
    per_rank = 2 * nb

    def rows(d, idx, b, c):
        return in_refs[d][idx][b, c * CH:(c + 1) * CH, :]
    heads = range(GDN_HEADS)
    pairs = range(GDN_HEADS // 2)

    r256 = lax.broadcasted_iota(jnp.int32, (CH, HC), 0)
    c256 = lax.broadcasted_iota(jnp.int32, (CH, HC), 1) & (CH - 1)
    eye = (c256 == r256).astype(F32)
    bd_mask = ((lax.broadcasted_iota(jnp.int32, (HC, HC), 0) >> 6)
               == (lax.broadcasted_iota(jnp.int32, (HC, HC), 1) >> 6)).astype(BF16)
    kbd_mask = ((lax.broadcasted_iota(jnp.int32, (HC, GDN_W), 0) >> 6)
                == (lax.broadcasted_iota(jnp.int32, (HC, GDN_W), 1) >> 7)).astype(BF16)
    r8 = lax.broadcasted_iota(jnp.int32, (2 * GDN_HEADS, HC), 0)
    h8 = lax.broadcasted_iota(jnp.int32, (2 * GDN_HEADS, HC), 1) >> 6

    st = []
    for d, b, c in groups:
        mk = masks[d]
        bg = rows(d, 3, b, c)
        gt = in_refs[d][4][b, c]
        cs = _dot(mk["tri3"], jnp.concatenate(_split3(bg), axis=0))
        cst = _dot(jnp.concatenate(_split3(gt), axis=1), mk["trit3"])
        st.append(dict(bg=bg, cs=cs, cst=cst))

    for g, (d, b, c) in enumerate(groups):
        mk, e = masks[d], st[g]
        crow = jnp.sum(jnp.where(r8 == d * GDN_HEADS + h8, e["cst"], 0.0), axis=0, keepdims=True)
        goff = 2 * GDN_HEADS + d * GDN_HEADS
        ccols = [e["cs"][:, goff + h:goff + h + 1] for h in heads]
        betas = [e["bg"][:, d * GDN_HEADS + h:d * GDN_HEADS + h + 1] for h in heads]
        e["ccols"] = ccols
        e["ccol512"] = _bcast_cols(ccols, GDN_D)
        e["beta512"] = _bcast_cols(betas, GDN_D)
        e["decay"] = jnp.exp(jnp.where(mk["incl"], _bcast_cols(ccols, CH) - crow, -jnp.inf))

    for g, (d, b, c) in enumerate(groups):
        mk, e = masks[d], st[g]
        q = rows(d, 0, b, c)
        k = rows(d, 1, b, c)
        kb = k * e["beta512"]
        kbd = jnp.concatenate([k.astype(BF16)] * GDN_HEADS, axis=0) * kbd_mask
        kq = lax.dot_general(jnp.concatenate([kb, q], axis=0).astype(BF16), kbd, NT_DIMS,
                             preferred_element_type=F32)
        neg_l = jnp.where(mk["strict"], -(kq[:CH] * e["decay"]), 0.0)
        e["intra"] = (kq[CH:] * e["decay"]).astype(BF16)
        e["p"] = eye + neg_l
        e["lm"] = neg_l

    for level in range(6):
        for e in st:
            w_bd = jnp.concatenate([e["lm"].astype(BF16)] * GDN_HEADS, axis=0) * bd_mask
            if level == 0:
                e["lm"] = _dot(e["lm"].astype(BF16), w_bd)
            elif level < 5:
                y = _dot(jnp.concatenate([e["p"], e["lm"]], axis=0).astype(BF16), w_bd)
                e["p"] = e["p"] + y[:CH]
                e["lm"] = y[CH:]
            else:
                e["p"] = e["p"] + _dot(e["p"].astype(BF16), w_bd)

    for g, (d, b, c) in enumerate(groups):
        e = st[g]
        k = rows(d, 1, b, c)
        v = rows(d, 2, b, c)
        tinv = e["p"].astype(BF16)
        vb = (v * e["beta512"]).astype(BF16)
        kbe = (k * e["beta512"] * jnp.exp(e["ccol512"])).astype(BF16)
        e["uw"] = []
        for pr in pairs:
            h0, h1 = 2 * pr, 2 * pr + 1
            rhs = lambda h: jnp.concatenate([vb[:, h * GDN_D:(h + 1) * GDN_D], kbe[:, h * GDN_D:(h + 1) * GDN_D]], axis=1)
            e["uw"].append(_dot(tinv[:, pr * GDN_D:(pr + 1) * GDN_D], _pair_diag(rhs(h0), rhs(h1))))

    for r in range(ncs):
        _gdn_state_stage(st, groups, r * per_rank, (r + 1) * per_rank, rows, out_refs, masks, s_ref)


def _gdn_state_stage(st, groups, lo, hi, rows, out_refs, masks, s_ref):
    pairs = range(GDN_HEADS // 2)
    for g in range(lo, hi):
        d, b, c = groups[g]
        e = st[g]
        q = rows(d, 0, b, c)
        qe = q * jnp.exp(e["ccol512"])
        e["ws"] = []
        for pr in pairs:
            lhs = []
            for h in (2 * pr, 2 * pr + 1):
                w = e["uw"][pr][:, (2 * (h % 2) + 1) * GDN_D:(2 * (h % 2) + 2) * GDN_D]
                lhs.append(jnp.concatenate([w, qe[:, h * GDN_D:(h + 1) * GDN_D]], axis=0))
            sidx = (b * 2 + d) * GDN_HEADS + 2 * pr
            sbd = _pair_diag(s_ref[sidx].astype(BF16), s_ref[sidx + 1].astype(BF16))
            e["ws"].append(_dot(jnp.concatenate(lhs, axis=1).astype(BF16), sbd))

    for g in range(lo, hi):
        d, b, c = groups[g]
        e = st[g]
        k = rows(d, 1, b, c)
        last = masks[d]["last"]
        for pr in pairs:
            vnew = []
            for h in (2 * pr, 2 * pr + 1):
                u = e["uw"][pr][:, 2 * (h % 2) * GDN_D:(2 * (h % 2) + 1) * GDN_D]
                vnew.append((u - e["ws"][pr][:CH, (h % 2) * GDN_D:(h % 2 + 1) * GDN_D]).astype(BF16))
            glast = [e["ccols"][h][last:last + 1, :] for h in (2 * pr, 2 * pr + 1)]
            kd = [k[:, h * GDN_D:(h + 1) * GDN_D] * jnp.exp(gl - e["ccol512"][:, h * GDN_D:(h + 1) * GDN_D])
                  for h, gl in zip((2 * pr, 2 * pr + 1), glast)]
            kdt = jnp.concatenate(kd, axis=0).T.astype(BF16)
            lhs = jnp.concatenate([e["intra"][:, pr * GDN_D:(pr + 1) * GDN_D], kdt], axis=0)
            prod = _dot(lhs, _pair_diag(vnew[0], vnew[1]))
            out_refs[d][b, c * CH:(c + 1) * CH, 2 * pr * GDN_D:(2 * pr + 2) * GDN_D] = (
                e["ws"][pr][CH:, :] + prod[:CH]).astype(BF16)
            for i, h in enumerate((2 * pr, 2 * pr + 1)):
                sidx = (b * 2 + d) * GDN_HEADS + h
                s_ref[sidx] = s_ref[sidx] * jnp.exp(glast[i]) + prod[CH:, i * GDN_D:(i + 1) * GDN_D]


def _gdn_scan(q, k, v, bg, gt, *, b, s):
    ncs = SCAN_CHUNKS if (s // CH) % SCAN_CHUNKS == 0 else 1
    n = s // (CH * ncs)
    fwd = lambda i: (0, i, 0)
    bwd = lambda i: (0, n - 1 - i, 0)
    fwd_t = lambda i: (0, i, 0, 0)
    bwd_t = lambda i: (0, n - 1 - i, 0, 0)
    wide = lambda im: pl.BlockSpec((b, ncs * CH, GDN_W), im)
    specs = lambda im, imt: [wide(im), wide(im), wide(im), pl.BlockSpec((b, ncs * CH, N_BA), im),
                             pl.BlockSpec((b, ncs, 2 * GDN_HEADS, CH), imt)]
    out = jax.ShapeDtypeStruct((b, s, GDN_W), BF16)
    return pl.pallas_call(
        _gdn_scan_body,
        out_shape=(out, out),
        grid=(n,),
        in_specs=specs(fwd, fwd_t) + specs(bwd, bwd_t),
        out_specs=(wide(fwd), wide(bwd)),
        scratch_shapes=[pltpu.VMEM((b * 2 * GDN_HEADS, GDN_D, GDN_D), F32)],
        compiler_params=_grid_params(1),
        name="gdn_scan",
    )(q, k, v, bg, gt, q, k, v, bg, gt)


def _merge_body(x_ref, of_ref, ob_ref, sz_ref, sg_ref, ot_ref, gnw_ref, wgp_ref, wmp_ref, wo_ref, o_ref):
    o = of_ref[...].astype(F32) + ob_ref[...].astype(F32)
    gnw = gnw_ref[...]
    heads = []
    for h in range(GDN_HEADS):
        oh = o[:, h * GDN_D:(h + 1) * GDN_D]
        heads.append(_rms(oh, gnw))
    on = jnp.concatenate(heads, axis=1) * sz_ref[...]
    ya = _dot(on.astype(BF16), wgp_ref[...])
    acc_t = ot_ref[0, :, 0].astype(F32)
    ot = (acc_t[:, :MLA_V, :] / acc_t[:, MLA_V:MLA_V + 1, :]).reshape(MLA_HEADS * MLA_V, -1)
    yb = _dot(ot.T.astype(BF16), wmp_ref[...])
    d = ya.shape[1]
    y = sg_ref[:, :d] * ya + sg_ref[:, d:] * yb
    o_ref[...] = x_ref[...] + _dot(y.astype(BF16), wo_ref[...])


def _merge(x, of, ob, sz, sg, ot, gnw, wgp, wmp, wo, *, b, s, tm):
    t, d = x.shape
    nsteps = s // tm
    row = lambda n: pl.BlockSpec((tm, n), lambda bi, i: (bi * nsteps + i, 0))
    return pl.pallas_call(
        _merge_body,
        out_shape=jax.ShapeDtypeStruct((t, d), F32),
        grid=(b, nsteps),
        in_specs=[row(d), row(GDN_W), row(GDN_W), row(GDN_W), row(2 * d),
                  pl.BlockSpec((1, MLA_HEADS, 1, V_ROWS, tm), lambda bi, i: (bi, 0, i, 0, 0)),
                  _const_spec((1, GDN_D)), _const_spec(wgp.shape), _const_spec(wmp.shape), _const_spec(wo.shape)],
        out_specs=row(d),
        compiler_params=_grid_params(2),
        name="merge",
    )(x, of, ob, sz, sg, ot, gnw, wgp, wmp, wo)


def _pack_w_in(w):
    d = w.shape[0]
    zeros = lambda n: jnp.zeros((d, n), w.dtype)
    o = 4 * GDN_W
    ba = w[:, o:o + N_BA]
    o += N_BA
    cq = w[:, o:o + MLA_Q_LORA]
    o += MLA_Q_LORA
    ckv = w[:, o:o + MLA_KV_LORA]
    o += MLA_KV_LORA
    kr = w[:, o:o + MLA_ROPE]
    o += MLA_ROPE
    gates = w[:, o:]
    half = MLA_ROPE // 2
    tail = zeros(HEAD_LANES - MLA_NOPE - MLA_ROPE)
    kr_main = jnp.concatenate([zeros(MLA_NOPE), kr, tail], axis=1)
    kr_swap = jnp.concatenate([zeros(MLA_NOPE), kr[:, half:], kr[:, :half], tail], axis=1)
    packed = jnp.concatenate([w[:, :4 * GDN_W], ba, zeros(128 - N_BA), cq, ckv, kr_main, kr_swap, gates], axis=1)
    return packed.astype(BF16), ba.T.astype(BF16)


def _pack_w_uq(w):
    r = w.shape[0]
    qk = MLA_NOPE + MLA_ROPE
    half = MLA_ROPE // 2
    tail = jnp.zeros((r, HEAD_LANES - qk), w.dtype)
    znope = jnp.zeros((r, MLA_NOPE), w.dtype)
    main, swap = [], []
    for h in range(MLA_HEADS):
        nope = w[:, h * qk:h * qk + MLA_NOPE]
        rope = w[:, h * qk + MLA_NOPE:(h + 1) * qk]
        main += [nope, rope, tail]
        swap += [znope, rope[:, half:], rope[:, :half], tail]
    return jnp.concatenate(main, axis=1).T.astype(BF16), jnp.concatenate(swap, axis=1).T.astype(BF16)


def _pack_w_ukv(w):
    r = w.shape[0]
    hw = MLA_NOPE + MLA_V
    ks, vs = [], []
    for h in range(MLA_HEADS):
        ks += [w[:, h * hw:h * hw + MLA_NOPE], jnp.zeros((r, HEAD_LANES - MLA_NOPE), w.dtype)]
        vs += [w[:, h * hw + MLA_NOPE:(h + 1) * hw], jnp.zeros((r, V_ROWS - MLA_V), w.dtype)]
    return jnp.concatenate(ks, axis=1).astype(BF16), jnp.concatenate(vs, axis=1).T.astype(BF16)


def _ones_rows():
    idx = jnp.arange(MLA_HEADS * V_ROWS) % V_ROWS
    return (idx == MLA_V).astype(F32)[:, None]


def _lane_pad(v, lo, width):
    return jnp.zeros((1, width), v.dtype).at[0, lo:lo + v.shape[0]].set(v)


def kernel(x, positions, norm_ffn1, ffn1_w_gate, ffn1_w_up, ffn1_w_down, norm_mix, w_in, gdn_conv, gdn_A_log,
           gdn_dt_bias, gdn_norm, gdn_proj, mla_q_norm, mla_w_uq, mla_kv_norm, mla_w_ukv, mla_proj, w_out,
           norm_ffn2, ffn2_w_gate, ffn2_w_up, ffn2_w_down, final_norm):
    b, s, d = x.shape
    t = b * s
    depth = w_in.shape[0]
    tm = 512 if s % 2048 == 0 else 256
    ffn_tm = 512
    tq = tm
    tk = KEY_BLOCKS_PER_TILE * tm

    cos, sin, cost, sint = _rope_tables(positions, tm)
    ones = _ones_rows()
    xf = x.reshape(t, d)
    row = lambda v: v.reshape(1, -1)
    for l in range(depth):
        xf = _ffn(xf, row(norm_ffn1[l]), ffn1_w_gate[l].astype(BF16), ffn1_w_up[l].astype(BF16),
                  ffn1_w_down[l].astype(BF16), row(final_norm), final_norm=False, tm=ffn_tm)

        w_packed, w_bat = _pack_w_in(w_in[l])
        alog = gdn_A_log[l].reshape(-1)
        dtb = gdn_dt_bias[l].reshape(-1)
        qkv, sz, bg, gt, cqn, ckvn, kr, sg = _inproj(
            xf, row(norm_mix[l]), w_packed, w_bat, _lane_pad(alog, 2 * GDN_HEADS, 128),
            _lane_pad(dtb, 2 * GDN_HEADS, 128), alog[:, None], dtb[:, None], row(mla_q_norm[l]),
            row(mla_kv_norm[l]), cos, sin, tm=tm)

        qn, kn, vv = _gdn_prep(qkv, gdn_conv[l], s=s, tm=tm)
        seq = lambda a: a.reshape(b, s, a.shape[-1])
        gt_chunks = gt.reshape(2 * GDN_HEADS, b, s // CH, CH).transpose(1, 2, 0, 3)
        of, ob = _gdn_scan(seq(qn), seq(kn), seq(vv), seq(bg), gt_chunks, b=b, s=s)
        of, ob = of.reshape(t, GDN_W), ob.reshape(t, GDN_W)

        wqm, wqp = _pack_w_uq(mla_w_uq[l])
        wk, wvt = _pack_w_ukv(mla_w_ukv[l])
        qt, kk, vt = _mla_prep(cqn, ckvn, kr, cost, sint, wqm, wqp, wk, wvt, ones, b=b, s=s, tq=tq, tk=tk)
        ot = _attention(qt, kk, vt, b=b, s=s, tq=tq, tk=tk)

        xf = _merge(xf, of, ob, sz, sg, ot, row(gdn_norm[l]), gdn_proj[l].astype(BF16), mla_proj[l].astype(BF16),
                    w_out[l].astype(BF16), b=b, s=s, tm=tm)

        xf = _ffn(xf, row(norm_ffn2[l]), ffn2_w_gate[l].astype(BF16), ffn2_w_up[l].astype(BF16),
                  ffn2_w_down[l].astype(BF16), row(final_norm), final_norm=(l == depth - 1), tm=ffn_tm)
    return xf.reshape(b, s, d)
```
